```python
import math
import jax, jax.numpy as jnp
from jax import lax
import numpy as np

D_MODEL = 1024
BATCH = 4
SEQ = 8192
DEPTH = 1

MEM_LEN = 256

SB_HEADS = 8
SB_HEAD_DIM = D_MODEL // 16
SB_WIDTH = SB_HEADS * SB_HEAD_DIM
SB_BLOCK = 128

RET_HEADS = 4
RET_QK_DIM = D_MODEL // 8
RET_V_DIM = 2 * RET_QK_DIM
RET_QK_WIDTH = RET_HEADS * RET_QK_DIM
RET_V_WIDTH = RET_HEADS * RET_V_DIM
RET_CHUNK = 128
ROPE_BASE = 10000.0

N_BRANCHES = 2

OFF_SB_Q = 0
OFF_SB_K = OFF_SB_Q + SB_WIDTH
OFF_SB_V = OFF_SB_K + SB_WIDTH
OFF_RET_Q = OFF_SB_V + SB_WIDTH
OFF_RET_K = OFF_RET_Q + RET_QK_WIDTH
OFF_RET_V = OFF_RET_K + RET_QK_WIDTH
OFF_RET_G = OFF_RET_V + RET_V_WIDTH
OFF_GATE = OFF_RET_G + RET_V_WIDTH
IN_WIDTH = OFF_GATE + N_BRANCHES * D_MODEL

MEM_HEADS = 4
MEM_HEAD_DIM = D_MODEL // MEM_HEADS

FFN_HIDDEN = -(-8 * D_MODEL // (3 * 256)) * 256

DN_ALPHA = (2.0 * DEPTH) ** 0.25
DN_BETA = (8.0 * DEPTH) ** -0.25
LN_EPS = 1e-5

kernel_name = "hybrid_stickbreak_retention_deepnorm"


def layer_norm(x, g, b):
    xf = x.astype(jnp.float32)
    mu = jnp.mean(xf, -1, keepdims=True)
    var = jnp.mean(jnp.square(xf - mu), -1, keepdims=True)
    return ((xf - mu) * lax.rsqrt(var + LN_EPS)).astype(x.dtype) * g + b


def head_norm(x):
    xf = x.astype(jnp.float32)
    mu = jnp.mean(xf, -1, keepdims=True)
    var = jnp.mean(jnp.square(xf - mu), -1, keepdims=True)
    return (xf - mu) * lax.rsqrt(var + LN_EPS)


def rotary(x):
    S, d = x.shape[1], x.shape[-1]
    half = d // 2
    inv = 1.0 / (ROPE_BASE ** (jnp.arange(half, dtype=jnp.float32) / half))
    ang = jnp.arange(S, dtype=jnp.float32)[:, None] * inv[None, :]
    cos = jnp.cos(ang)[None, :, None, :]
    sin = jnp.sin(ang)[None, :, None, :]
    xf = x.astype(jnp.float32)
    x1, x2 = xf[..., :half], xf[..., half:]
    return jnp.concatenate([x1 * cos - x2 * sin, x1 * sin + x2 * cos], -1).astype(x.dtype)


def stick_breaking_attention(q, k, v):
    B, S, H, d = q.shape
    nb = S // SB_BLOCK
    qb = (q.astype(jnp.float32) * (d ** -0.5)).reshape(B, nb, SB_BLOCK, H, d).transpose(1, 0, 3, 2, 4)
    kf = k.astype(jnp.float32).transpose(0, 2, 1, 3)
    vf = v.astype(jnp.float32).transpose(0, 2, 1, 3)
    key_pos = jnp.arange(S)

    def one_block(args):
        q_blk, blk = args
        q_pos = blk * SB_BLOCK + jnp.arange(SB_BLOCK)
        mask = key_pos[None, :] < q_pos[:, None]
        z = jnp.einsum('bhqd,bhkd->bhqk', q_blk, kf)
        log_beta = jax.nn.log_sigmoid(z)
        log_rem = jnp.where(mask, jax.nn.log_sigmoid(-z), 0.0)
        later = lax.cumsum(log_rem, axis=3, reverse=True) - log_rem
        w = jnp.where(mask, jnp.exp(log_beta + later), 0.0)
        return jnp.einsum('bhqk,bhkd->bhqd', w, vf)

    out = lax.map(one_block, (qb, jnp.arange(nb)))
    return out.transpose(1, 0, 3, 2, 4).reshape(B, S, H * d)


def retention(q, k, v):
    B, S, H, dk = q.shape
    dv = v.shape[-1]
    nc = S // RET_CHUNK
    log_gamma = jnp.log1p(-jnp.exp2(-5.0 - jnp.arange(H, dtype=jnp.float32)))
    qc = (q.astype(jnp.float32) * (dk ** -0.5)).reshape(B, nc, RET_CHUNK, H, dk)
    kc = k.astype(jnp.float32).reshape(B, nc, RET_CHUNK, H, dk)
    vc = v.astype(jnp.float32).reshape(B, nc, RET_CHUNK, H, dv)
    idx = jnp.arange(RET_CHUNK, dtype=jnp.float32)
    rel = idx[:, None] - idx[None, :]
    decay = jnp.where(rel[None] >= 0,
                      jnp.exp(log_gamma[:, None, None] * jnp.maximum(rel, 0.0)[None]), 0.0)
    scores = jnp.einsum('bnihd,bnjhd->bnhij', qc, kc) * decay
    inner = jnp.einsum('bnhij,bnjhe->bnihe', scores, vc)
    k_decay = jnp.exp(log_gamma[None, :] * (RET_CHUNK - 1.0 - idx)[:, None])
    chunk_kv = jnp.einsum('bnjhd,jh,bnjhe->nbhde', kc, k_decay, vc)
    chunk_decay = jnp.exp(log_gamma * RET_CHUNK)[None, :, None, None]

    def step(state, kv):
        return state * chunk_decay + kv, state

    _, states = lax.scan(step, jnp.zeros((B, H, dk, dv), jnp.float32), chunk_kv)
    q_decay = jnp.exp(log_gamma[None, :] * (idx + 1.0)[:, None])
    cross = jnp.einsum('bnihd,ih,nbhde->bnihe', qc, q_decay, states)
    return (inner + cross).reshape(B, S, H, dv)


def memory_cross_attention(x, mem, w_q, w_kv, w_o):
    B, S, _ = x.shape
    q = (x @ w_q).reshape(B, S, MEM_HEADS, MEM_HEAD_DIM)
    kv = (mem @ w_kv).reshape(B, MEM_LEN, 2, MEM_HEADS, MEM_HEAD_DIM)
    k, v = kv[:, :, 0], kv[:, :, 1]
    s = jnp.einsum('bshd,bmhd->bhsm', q.astype(jnp.float32), k.astype(jnp.float32)) * (MEM_HEAD_DIM ** -0.5)
    p = jax.nn.softmax(s, axis=-1)
    o = jnp.einsum('bhsm,bmhd->bshd', p, v.astype(jnp.float32)).reshape(B, S, D_MODEL)
    return o.astype(x.dtype) @ w_o


def setup_inputs(seed: int = 0) -> dict:
    key = jax.random.key(seed)
    ks = jax.random.split(key, 24)
    nrm = lambda k, shape: jax.random.normal(k, shape, jnp.float32)

    def dense(k, fan_in, fan_out, scale=1.0):
        return nrm(k, (DEPTH, fan_in, fan_out)) * (fan_in ** -0.5) * scale

    gain = lambda k: 1.0 + 0.02 * nrm(k, (DEPTH, D_MODEL))
    bias = lambda k, n: 0.02 * nrm(k, (DEPTH, n))

    col_scale = jnp.concatenate([
        jnp.ones((OFF_SB_V,), jnp.float32),
        jnp.full((SB_WIDTH,), DN_BETA, jnp.float32),
        jnp.ones((2 * RET_QK_WIDTH,), jnp.float32),
        jnp.full((RET_V_WIDTH,), DN_BETA, jnp.float32),
        jnp.ones((RET_V_WIDTH + N_BRANCHES * D_MODEL,), jnp.float32)])
    mem_kv_scale = jnp.concatenate([jnp.ones((D_MODEL,), jnp.float32),
                                    jnp.full((D_MODEL,), DN_BETA, jnp.float32)])
    return {
        "x": nrm(ks[0], (BATCH, SEQ, D_MODEL)),
        "mem": nrm(ks[1], (BATCH, MEM_LEN, D_MODEL)),
        "w_in": dense(ks[2], D_MODEL, IN_WIDTH) * col_scale,
        "b_gate": bias(ks[3], N_BRANCHES * D_MODEL),
        "w_sb_o": dense(ks[4], SB_WIDTH, D_MODEL),
        "w_ret_o": dense(ks[5], RET_V_WIDTH, D_MODEL),
        "w_mix_o": dense(ks[6], D_MODEL, D_MODEL, DN_BETA),
        "ln1_g": gain(ks[7]),
        "ln1_b": bias(ks[8], D_MODEL),
        "w_mem_q": dense(ks[9], D_MODEL, D_MODEL),
        "w_mem_kv": dense(ks[10], D_MODEL, 2 * D_MODEL) * mem_kv_scale,
        "w_mem_o": dense(ks[11], D_MODEL, D_MODEL, DN_BETA),
        "ln2_g": gain(ks[12]),
        "ln2_b": bias(ks[13], D_MODEL),
        "w_ffn_in": dense(ks[14], D_MODEL, 2 * FFN_HIDDEN, DN_BETA),
        "w_ffn_out": dense(ks[15], FFN_HIDDEN, D_MODEL, DN_BETA),
        "ln3_g": gain(ks[16]),
        "ln3_b": bias(ks[17], D_MODEL),
    }


def reference(x, mem, w_in, b_gate, w_sb_o, w_ret_o, w_mix_o, ln1_g, ln1_b,
              w_mem_q, w_mem_kv, w_mem_o, ln2_g, ln2_b, w_ffn_in, w_ffn_out, ln3_g, ln3_b):
    B, S, _ = x.shape
    for l in range(DEPTH):
        h = x @ w_in[l]
        sb_q = h[..., OFF_SB_Q:OFF_SB_K].reshape(B, S, SB_HEADS, SB_HEAD_DIM)
        sb_k = h[..., OFF_SB_K:OFF_SB_V].reshape(B, S, SB_HEADS, SB_HEAD_DIM)
        sb_v = h[..., OFF_SB_V:OFF_RET_Q].reshape(B, S, SB_HEADS, SB_HEAD_DIM)
        r_q = rotary(h[..., OFF_RET_Q:OFF_RET_K].reshape(B, S, RET_HEADS, RET_QK_DIM))
        r_k = rotary(h[..., OFF_RET_K:OFF_RET_V].reshape(B, S, RET_HEADS, RET_QK_DIM))
        r_v = h[..., OFF_RET_V:OFF_RET_G].reshape(B, S, RET_HEADS, RET_V_DIM)
        r_g = h[..., OFF_RET_G:OFF_GATE]
        gates = jax.nn.sigmoid(h[..., OFF_GATE:] + b_gate[l]).reshape(B, S, N_BRANCHES, D_MODEL)

        y_sb = stick_breaking_attention(sb_q, sb_k, sb_v).astype(x.dtype) @ w_sb_o[l]
        ret = head_norm(retention(r_q, r_k, r_v)).reshape(B, S, RET_V_WIDTH)
        y_ret = (jax.nn.silu(r_g) * ret.astype(x.dtype)) @ w_ret_o[l]
        mix = (gates[:, :, 0] * y_sb + gates[:, :, 1] * y_ret) @ w_mix_o[l]
        x = layer_norm(DN_ALPHA * x + mix, ln1_g[l], ln1_b[l])

        xa = memory_cross_attention(x, mem, w_mem_q[l], w_mem_kv[l], w_mem_o[l])
        x = layer_norm(DN_ALPHA * x + xa, ln2_g[l], ln2_b[l])

        f = x @ w_ffn_in[l]
        ff = (jax.nn.silu(f[..., :FFN_HIDDEN]) * f[..., FFN_HIDDEN:]) @ w_ffn_out[l]
        x = layer_norm(DN_ALPHA * x + ff, ln3_g[l], ln3_b[l])
    return x
```

```python
import functools

import jax
import jax.numpy as jnp
from jax import lax
from jax.experimental import pallas as pl
from jax.experimental.pallas import tpu as pltpu

F32 = jnp.float32
BF16 = jnp.bfloat16

D_MODEL = 1024
DEPTH = 1
SB_HEADS = 8
SB_HEAD_DIM = 64
SB_WIDTH = SB_HEADS * SB_HEAD_DIM
RET_HEADS = 4
RET_QK_DIM = 128
RET_V_DIM = 256
RET_QK_WIDTH = RET_HEADS * RET_QK_DIM
RET_V_WIDTH = RET_HEADS * RET_V_DIM
ROPE_BASE = 10000.0
N_BRANCHES = 2
OFF_SB_Q = 0
OFF_SB_K = OFF_SB_Q + SB_WIDTH
OFF_SB_V = OFF_SB_K + SB_WIDTH
OFF_RET_Q = OFF_SB_V + SB_WIDTH
OFF_RET_K = OFF_RET_Q + RET_QK_WIDTH
OFF_RET_V = OFF_RET_K + RET_QK_WIDTH
OFF_RET_G = OFF_RET_V + RET_V_WIDTH
OFF_GATE = OFF_RET_G + RET_V_WIDTH
IN_WIDTH = OFF_GATE + N_BRANCHES * D_MODEL
MEM_HEADS = 4
MEM_HEAD_DIM = D_MODEL // MEM_HEADS
FFN_HIDDEN = 2816
DN_ALPHA = (2.0 * DEPTH) ** 0.25
LN_EPS = 1e-5

LANES = 128
VMEM_LIMIT_BYTES = 56 * 1024 * 1024
ROW_TILE = 512
SB_TILE = 128
RET_CHUNK = 256
RET_ROWS = 1024
FFN_CHUNK = 256


def _resident(shape):
    zeros = (0,) * len(shape)
    return pl.BlockSpec(shape, lambda *_: zeros, pipeline_mode=pl.Buffered(1))


def _params(*sem):
    return pltpu.CompilerParams(dimension_semantics=sem, vmem_limit_bytes=VMEM_LIMIT_BYTES)


def _layer_norm(y, g, b):
    mu = jnp.mean(y, axis=-1, keepdims=True)
    yc = y - mu
    var = jnp.mean(yc * yc, axis=-1, keepdims=True)
    return yc * lax.rsqrt(var + LN_EPS) * g + b


def _inproj_kernel(x_ref, w_ref, bg_ref, cos_ref, sin_ref,
                   sbq_ref, sbk_ref, sbv_ref, rq_ref, rk_ref, rv_ref, rg_ref, gate_ref):
    xb = x_ref[...].astype(BF16)

    def proj(off, width):
        return jnp.dot(xb, w_ref[:, off:off + width], preferred_element_type=F32)

    sbq_ref[...] = (proj(OFF_SB_Q, SB_WIDTH) * (SB_HEAD_DIM ** -0.5)).astype(BF16)
    sbk_ref[...] = proj(OFF_SB_K, SB_WIDTH).astype(BF16)
    sbv_ref[...] = proj(OFF_SB_V, SB_WIDTH).astype(BF16)

    cos = cos_ref[...]
    sin = sin_ref[...]

    def rope_store(h, out_ref, scale):
        for hd in range(RET_HEADS):
            sl = slice(hd * RET_QK_DIM, (hd + 1) * RET_QK_DIM)
            xh = h[:, sl]
            rot = xh * cos + pltpu.roll(xh, RET_QK_DIM // 2, 1) * sin
            if scale != 1.0:
                rot = rot * scale
            out_ref[:, sl] = rot.astype(BF16)

    rope_store(proj(OFF_RET_Q, RET_QK_WIDTH), rq_ref, RET_QK_DIM ** -0.5)
    rope_store(proj(OFF_RET_K, RET_QK_WIDTH), rk_ref, 1.0)
    rv_ref[...] = proj(OFF_RET_V, RET_V_WIDTH).astype(BF16)
    g = proj(OFF_RET_G, RET_V_WIDTH)
    rg_ref[...] = (g * jax.nn.sigmoid(g)).astype(BF16)
    for br in range(N_BRANCHES):
        sl = slice(br * D_MODEL, (br + 1) * D_MODEL)
        hg = proj(OFF_GATE + br * D_MODEL, D_MODEL) + bg_ref[:, sl]
        gate_ref[:, sl] = jax.nn.sigmoid(hg).astype(BF16)


def _inproj(x2d, w_in, b_gate, cos_t, sin_t, seq):
    T = x2d.shape[0]
    tm = ROW_TILE
    tiles_per_seq = seq // tm
    row = lambda w: pl.BlockSpec((tm, w), lambda i: (i, 0))
    tab = pl.BlockSpec((tm, LANES), lambda i: (i % tiles_per_seq, 0))
    widths = (SB_WIDTH, SB_WIDTH, SB_WIDTH, RET_QK_WIDTH, RET_QK_WIDTH,
              RET_V_WIDTH, RET_V_WIDTH, N_BRANCHES * D_MODEL)
    return pl.pallas_call(
        _inproj_kernel,
        grid=(T // tm,),
        in_specs=[row(D_MODEL), _resident((D_MODEL, IN_WIDTH)),
                  _resident((1, N_BRANCHES * D_MODEL)), tab, tab],
        out_specs=[row(w) for w in widths],
        out_shape=[jax.ShapeDtypeStruct((T, w), BF16) for w in widths],
        compiler_params=_params("parallel"),
        name="inproj",
    )(x2d, w_in, b_gate, cos_t, sin_t)


def _sb_kernel(q_ref, k_ref, v_ref, cum_ref, o_ref):
    qi = pl.program_id(2)
    t = SB_TILE
    q2 = q_ref[0]
    lane = lax.broadcasted_iota(jnp.int32, (t, LANES), 1)
    first = lane < SB_HEAD_DIM
    q_heads = (jnp.where(first, q2, jnp.zeros_like(q2)), jnp.where(first, jnp.zeros_like(q2), q2))
    cum = cum_ref[...]
    row = lax.broadcasted_iota(jnp.int32, (t, t), 0)
    col = lax.broadcasted_iota(jnp.int32, (t, t), 1)
    causal = col < row

    def tile(qh, kj, vj, run, acc, mask):
        z = lax.dot_general(qh, kj, (((1,), (1,)), ((), ())), preferred_element_type=F32)
        log_rem = -(jnp.maximum(z, 0.0) + jnp.log(1.0 + jnp.exp(-jnp.abs(z))))
        log_beta = z + log_rem
        if mask is not None:
            log_rem = jnp.where(mask, log_rem, 0.0)
        hi = log_rem.astype(BF16)
        lo = (log_rem - hi.astype(F32)).astype(BF16)
        sums = jnp.dot(jnp.concatenate([hi, lo], axis=1), cum, preferred_element_type=F32)
        later = sums[:, :t]
        w = jnp.exp(log_beta + later + run)
        if mask is not None:
            w = jnp.where(mask, w, 0.0)
        acc = acc + jnp.dot(w.astype(BF16), vj, preferred_element_type=F32)
        return run + sums[:, t:], acc

    def kv(j):
        start = pl.multiple_of(j * t, t)
        return k_ref[0, pl.ds(start, t), :], v_ref[0, pl.ds(start, t), :]

    zero = jnp.zeros((t, LANES), F32)
    kd, vd = kv(qi)
    state = []
    for qh in q_heads:
        state.extend(tile(qh, kd, vd, zero, zero, causal))

    def body(s, carry):
        kj, vj = kv(qi - 1 - s)
        ra, aa, rb, ab = carry
        ra, aa = tile(q_heads[0], kj, vj, ra, aa, None)
        rb, ab = tile(q_heads[1], kj, vj, rb, ab, None)
        return ra, aa, rb, ab

    _, acc_a, _, acc_b = lax.fori_loop(0, qi, body, tuple(state))
    o_ref[0] = jnp.where(first, acc_a, acc_b).astype(BF16)


def _sb_attention(q, k, v, cum):
    B, S, _ = q.shape
    t = SB_TILE
    pairs = SB_WIDTH // LANES
    qspec = pl.BlockSpec((1, t, LANES), lambda b, p, i: (b, i, p))
    kvspec = pl.BlockSpec((1, S, LANES), lambda b, p, i: (b, 0, p))
    return pl.pallas_call(
        _sb_kernel,
        grid=(B, pairs, S // t),
        in_specs=[qspec, kvspec, kvspec, _resident((2 * t, 2 * t))],
        out_specs=qspec,
        out_shape=jax.ShapeDtypeStruct((B, S, SB_WIDTH), BF16),
        compiler_params=_params("parallel", "parallel", "arbitrary"),
        name="sb_attn",
    )(q, k, v, cum)


def _ret_kernel(q_ref, k_ref, v_ref, g_ref, decay_ref, qd_ref, kd_ref, cd_ref, o_ref, state_ref):
    @pl.when(pl.program_id(2) == 0)
    def _():
        state_ref[...] = jnp.zeros_like(state_ref)

    c = RET_CHUNK
    for ci in range(RET_ROWS // c):
        sl = slice(ci * c, (ci + 1) * c)
        q = q_ref[0, sl, :]
        k = k_ref[0, sl, :]
        v = v_ref[0, sl, :]
        s = lax.dot_general(q, k, (((1,), (1,)), ((), ())), preferred_element_type=F32) * decay_ref[0]
        inner = jnp.dot(s.astype(BF16), v, preferred_element_type=F32)
        st = state_ref[...]
        cross = jnp.dot(q, st.astype(BF16), preferred_element_type=F32) * qd_ref[0]
        o = inner + cross
        vdec = (v.astype(F32) * kd_ref[0]).astype(BF16)
        state_ref[...] = st * cd_ref[0] + lax.dot_general(
            k, vdec, (((0,), (0,)), ((), ())), preferred_element_type=F32)
        mu = jnp.mean(o, axis=-1, keepdims=True)
        oc = o - mu
        var = jnp.mean(oc * oc, axis=-1, keepdims=True)
        normed = oc * lax.rsqrt(var + LN_EPS)
        o_ref[0, sl, :] = (g_ref[0, sl, :].astype(F32) * normed).astype(BF16)


def _retention(rq, rk, rv, rg, tables):
    B, S, _ = rq.shape
    c = RET_CHUNK
    qk = pl.BlockSpec((1, RET_ROWS, RET_QK_DIM), lambda b, h, r: (b, r, h))
    vv = pl.BlockSpec((1, RET_ROWS, RET_V_DIM), lambda b, h, r: (b, r, h))
    head = lambda rows, cols: pl.BlockSpec((1, rows, cols), lambda b, h, r: (h, 0, 0))
    return pl.pallas_call(
        _ret_kernel,
        grid=(B, RET_HEADS, S // RET_ROWS),
        in_specs=[qk, qk, vv, vv, head(c, c), head(c, RET_V_DIM), head(c, RET_V_DIM),
                  head(RET_QK_DIM, RET_V_DIM)],
        out_specs=vv,
        out_shape=jax.ShapeDtypeStruct((B, S, RET_V_WIDTH), BF16),
        scratch_shapes=[pltpu.VMEM((RET_QK_DIM, RET_V_DIM), F32)],
        compiler_params=_params("parallel", "parallel", "arbitrary"),
        name="retention",
    )(rq, rk, rv, rg, *tables)


def _retention_tables():
    c = RET_CHUNK
    log_gamma = jnp.log1p(-jnp.exp2(-5.0 - jnp.arange(RET_HEADS, dtype=F32)))
    idx = jnp.arange(c, dtype=F32)
    rel = idx[:, None] - idx[None, :]
    lg = log_gamma[:, None, None]
    decay = jnp.where(rel[None] >= 0, jnp.exp(lg * jnp.maximum(rel, 0.0)[None]), 0.0)
    qd = jnp.broadcast_to(jnp.exp(lg * (idx + 1.0)[None, :, None]), (RET_HEADS, c, RET_V_DIM))
    kd = jnp.broadcast_to(jnp.exp(lg * (c - 1.0 - idx)[None, :, None]), (RET_HEADS, c, RET_V_DIM))
    cd = jnp.broadcast_to(jnp.exp(lg * float(c)), (RET_HEADS, RET_QK_DIM, RET_V_DIM))
    return decay, qd, kd, cd


def _mix_kernel(a_ref, r_ref, gate_ref, x_ref, wsb_ref, wret_ref, wmix_ref, g_ref, b_ref, o_ref):
    y_sb = jnp.dot(a_ref[...], wsb_ref[...], preferred_element_type=F32)
    y_ret = jnp.dot(r_ref[...], wret_ref[...], preferred_element_type=F32)
    merged = (gate_ref[:, :D_MODEL].astype(F32) * y_sb
              + gate_ref[:, D_MODEL:].astype(F32) * y_ret)
    mix = jnp.dot(merged.astype(BF16), wmix_ref[...], preferred_element_type=F32)
    o_ref[...] = _layer_norm(DN_ALPHA * x_ref[...] + mix, g_ref[...], b_ref[...])


def _mix_ln1(attn, retg, gates, x2d, w_sb_o, w_ret_o, w_mix_o, g, b):
    T = x2d.shape[0]
    tm = ROW_TILE
    row = lambda w: pl.BlockSpec((tm, w), lambda i: (i, 0))
    return pl.pallas_call(
        _mix_kernel,
        grid=(T // tm,),
        in_specs=[row(SB_WIDTH), row(RET_V_WIDTH), row(N_BRANCHES * D_MODEL), row(D_MODEL),
                  _resident(w_sb_o.shape), _resident(w_ret_o.shape), _resident(w_mix_o.shape),
                  _resident(g.shape), _resident(b.shape)],
        out_specs=row(D_MODEL),
        out_shape=jax.ShapeDtypeStruct((T, D_MODEL), F32),
        compiler_params=_params("parallel"),
        name="mix_ln1",
    )(attn, retg, gates, x2d, w_sb_o, w_ret_o, w_mix_o, g, b)


def _memkv_kernel(m_ref, w_ref, k_ref, v_ref):
    mb = m_ref[0].astype(BF16)
    k_ref[0] = jnp.dot(mb, w_ref[:, :D_MODEL], preferred_element_type=F32).astype(BF16)
    v_ref[0] = jnp.dot(mb, w_ref[:, D_MODEL:], preferred_element_type=F32).astype(BF16)


def _mem_kv(mem, w_kv):
    B, M, _ = mem.shape
    blk = pl.BlockSpec((1, M, D_MODEL), lambda b: (b, 0, 0))
    return pl.pallas_call(
        _memkv_kernel,
        grid=(B,),
        in_specs=[blk, _resident(w_kv.shape)],
        out_specs=[blk, blk],
        out_shape=[jax.ShapeDtypeStruct((B, M, D_MODEL), BF16)] * 2,
        compiler_params=_params("parallel"),
        name="mem_kv",
    )(mem, w_kv)


def _xattn_kernel(x_ref, k_ref, v_ref, wq_ref, wo_ref, g_ref, b_ref, o_ref, ctx_ref):
    x = x_ref[0]
    q = (jnp.dot(x.astype(BF16), wq_ref[...], preferred_element_type=F32)
         * (MEM_HEAD_DIM ** -0.5)).astype(BF16)
    for h in range(MEM_HEADS):
        sl = slice(h * MEM_HEAD_DIM, (h + 1) * MEM_HEAD_DIM)
        s = lax.dot_general(q[:, sl], k_ref[0, :, sl], (((1,), (1,)), ((), ())),
                            preferred_element_type=F32)
        p = jnp.exp(s - jnp.max(s, axis=-1, keepdims=True))
        denom = jnp.sum(p, axis=-1, keepdims=True)
        ctx = jnp.dot(p.astype(BF16), v_ref[0, :, sl], preferred_element_type=F32)
        ctx_ref[:, sl] = (ctx / denom).astype(BF16)
    xa = jnp.dot(ctx_ref[...], wo_ref[...], preferred_element_type=F32)
    o_ref[0] = _layer_norm(DN_ALPHA * x + xa, g_ref[...], b_ref[...])


def _xattn_ln2(x3d, mk, mv, w_q, w_o, g, b):
    B, S, _ = x3d.shape
    M = mk.shape[1]
    tm = ROW_TILE
    row = pl.BlockSpec((1, tm, D_MODEL), lambda bi, i: (bi, i, 0))
    kv = pl.BlockSpec((1, M, D_MODEL), lambda bi, i: (bi, 0, 0))
    return pl.pallas_call(
        _xattn_kernel,
        grid=(B, S // tm),
        in_specs=[row, kv, kv, _resident(w_q.shape), _resident(w_o.shape),
                  _resident(g.shape), _resident(b.shape)],
        out_specs=row,
        out_shape=jax.ShapeDtypeStruct((B, S, D_MODEL), F32),
        scratch_shapes=[pltpu.VMEM((tm, D_MODEL), BF16)],
        compiler_params=_params("parallel", "arbitrary"),
        name="xattn_ln2",
    )(x3d, mk, mv, w_q, w_o, g, b)


def _ffn_kernel(x_ref, win_ref, wout_ref, g_ref, b_ref, o_ref, hid_ref):
    x = x_ref[...]
    xb = x.astype(BF16)
    for c in range(FFN_HIDDEN // FFN_CHUNK):
        lo = c * FFN_CHUNK
        a = jnp.dot(xb, win_ref[:, lo:lo + FFN_CHUNK], preferred_element_type=F32)
        gte = jnp.dot(xb, win_ref[:, FFN_HIDDEN + lo:FFN_HIDDEN + lo + FFN_CHUNK],
                      preferred_element_type=F32)
        hid_ref[:, lo:lo + FFN_CHUNK] = (a * jax.nn.sigmoid(a) * gte).astype(BF16)
    ff = jnp.dot(hid_ref[...], wout_ref[...], preferred_element_type=F32)
    o_ref[...] = _layer_norm(DN_ALPHA * x + ff, g_ref[...], b_ref[...])


def _ffn_ln3(x2d, w_in, w_out, g, b):
    T = x2d.shape[0]
    tm = ROW_TILE
    row = pl.BlockSpec((tm, D_MODEL), lambda i: (i, 0))
    return pl.pallas_call(
        _ffn_kernel,
        grid=(T // tm,),
        in_specs=[row, _resident(w_in.shape), _resident(w_out.shape),
                  _resident(g.shape), _resident(b.shape)],
        out_specs=row,
        out_shape=jax.ShapeDtypeStruct((T, D_MODEL), F32),
        scratch_shapes=[pltpu.VMEM((tm, FFN_HIDDEN), BF16)],
        compiler_params=_params("parallel"),
        name="ffn_ln3",
    )(x2d, w_in, w_out, g, b)


def _rope_tables(seq):
    half = RET_QK_DIM // 2
    inv = 1.0 / (ROPE_BASE ** (jnp.arange(half, dtype=F32) / half))
    ang = jnp.arange(seq, dtype=F32)[:, None] * inv[None, :]
    cos = jnp.cos(ang)
    sin = jnp.sin(ang)
    return jnp.concatenate([cos, cos], axis=-1), jnp.concatenate([-sin, sin], axis=-1)


def _cumsum_matrix():
    t = SB_TILE
    r = jnp.arange(t)
    upper = (r[:, None] > r[None, :]).astype(BF16)
    half = jnp.concatenate([upper, jnp.ones((t, t), BF16)], axis=1)
    return jnp.concatenate([half, half], axis=0)


def kernel(x, mem, w_in, b_gate, w_sb_o, w_ret_o, w_mix_o, ln1_g, ln1_b, w_mem_q, w_mem_kv,
           w_mem_o, ln2_g, ln2_b, w_ffn_in, w_ffn_out, ln3_g, ln3_b):
    B, S, D = x.shape
    assert D == D_MODEL and w_in.shape == (DEPTH, D_MODEL, IN_WIDTH)
    assert S % RET_ROWS == 0 and (B * S) % ROW_TILE == 0 and S % ROW_TILE == 0
    cos_t, sin_t = _rope_tables(S)
    cum = _cumsum_matrix()
    ret_tables = _retention_tables()
    x2d = x.reshape(B * S, D)
    for l in range(DEPTH):
        bf = lambda w: w[l].astype(BF16)
        vec = lambda p: p[l][None, :]
        sbq, sbk, sbv, rq, rk, rv, rg, gates = _inproj(
            x2d, bf(w_in), vec(b_gate), cos_t, sin_t, S)
        seq3 = lambda a: a.reshape(B, S, a.shape[-1])
        attn = _sb_attention(seq3(sbq), seq3(sbk), seq3(sbv), cum)
        retg = _retention(seq3(rq), seq3(rk), seq3(rv), seq3(rg), ret_tables)
        x1 = _mix_ln1(attn.reshape(B * S, SB_WIDTH), retg.reshape(B * S, RET_V_WIDTH), gates, x2d,
                      bf(w_sb_o), bf(w_ret_o), bf(w_mix_o), vec(ln1_g), vec(ln1_b))
        mk, mv = _mem_kv(mem, bf(w_mem_kv))
        x2 = _xattn_ln2(x1.reshape(B, S, D), mk, mv, bf(w_mem_q), bf(w_mem_o),
                        vec(ln2_g), vec(ln2_b))
        x2d = _ffn_ln3(x2.reshape(B * S, D), bf(w_ffn_in), bf(w_ffn_out), vec(ln3_g), vec(ln3_b))
    return x2d.reshape(B, S, D)
```

```python
import functools

import jax
import jax.numpy as jnp
from jax import lax
from jax.experimental import pallas as pl
from jax.experimental.pallas import tpu as pltpu

F32 = jnp.float32
BF16 = jnp.bfloat16

D_MODEL = 1024
DEPTH = 1
SB_HEADS = 8
SB_HEAD_DIM = 64
SB_WIDTH = SB_HEADS * SB_HEAD_DIM
RET_HEADS = 4
RET_QK_DIM = 128
RET_V_DIM = 256
RET_QK_WIDTH = RET_HEADS * RET_QK_DIM
RET_V_WIDTH = RET_HEADS * RET_V_DIM
ROPE_BASE = 10000.0
N_BRANCHES = 2
OFF_SB_Q = 0
OFF_SB_K = OFF_SB_Q + SB_WIDTH
OFF_SB_V = OFF_SB_K + SB_WIDTH
OFF_RET_Q = OFF_SB_V + SB_WIDTH
OFF_RET_K = OFF_RET_Q + RET_QK_WIDTH
OFF_RET_V = OFF_RET_K + RET_QK_WIDTH
OFF_RET_G = OFF_RET_V + RET_V_WIDTH
OFF_GATE = OFF_RET_G + RET_V_WIDTH
IN_WIDTH = OFF_GATE + N_BRANCHES * D_MODEL
MEM_HEADS = 4
MEM_HEAD_DIM = D_MODEL // MEM_HEADS
FFN_HIDDEN = 2816
DN_ALPHA = (2.0 * DEPTH) ** 0.25
LN_EPS = 1e-5

LANES = 128
VMEM_LIMIT_BYTES = 56 * 1024 * 1024
ROW_TILE = 512
SB_TILE = 128
SB_QTILES = 4
SB_STEPS_PER_CHECK = 2
SB_UNDERFLOW = -104.0
SB_GONE = -1e30
RET_CHUNK = 256
RET_ROWS = 1024
FFN_CHUNK = 256


def _resident(shape):
    zeros = (0,) * len(shape)
    return pl.BlockSpec(shape, lambda *_: zeros, pipeline_mode=pl.Buffered(1))


def _params(*sem):
    return pltpu.CompilerParams(dimension_semantics=sem, vmem_limit_bytes=VMEM_LIMIT_BYTES)


def _layer_norm(y, g, b):
    mu = jnp.mean(y, axis=-1, keepdims=True)
    yc = y - mu
    var = jnp.mean(yc * yc, axis=-1, keepdims=True)
    return yc * lax.rsqrt(var + LN_EPS) * g + b


def _inproj_kernel(x_ref, w_ref, bg_ref, cos_ref, sin_ref,
                   sbq_ref, sbk_ref, sbv_ref, rq_ref, rk_ref, rv_ref, rg_ref, gate_ref):
    xb = x_ref[...].astype(BF16)

    def proj(off, width):
        return jnp.dot(xb, w_ref[:, off:off + width], preferred_element_type=F32)

    sbq_ref[...] = (proj(OFF_SB_Q, SB_WIDTH) * (SB_HEAD_DIM ** -0.5)).astype(BF16)
    sbk_ref[...] = proj(OFF_SB_K, SB_WIDTH).astype(BF16)
    sbv_ref[...] = proj(OFF_SB_V, SB_WIDTH).astype(BF16)

    cos = cos_ref[...]
    sin = sin_ref[...]

    def rope_store(h, out_ref, scale):
        for hd in range(RET_HEADS):
            sl = slice(hd * RET_QK_DIM, (hd + 1) * RET_QK_DIM)
            xh = h[:, sl]
            rot = xh * cos + pltpu.roll(xh, RET_QK_DIM // 2, 1) * sin
            if scale != 1.0:
                rot = rot * scale
            out_ref[:, sl] = rot.astype(BF16)

    rope_store(proj(OFF_RET_Q, RET_QK_WIDTH), rq_ref, RET_QK_DIM ** -0.5)
    rope_store(proj(OFF_RET_K, RET_QK_WIDTH), rk_ref, 1.0)
    rv_ref[...] = proj(OFF_RET_V, RET_V_WIDTH).astype(BF16)
    g = proj(OFF_RET_G, RET_V_WIDTH)
    rg_ref[...] = (g * jax.nn.sigmoid(g)).astype(BF16)
    for br in range(N_BRANCHES):
        sl = slice(br * D_MODEL, (br + 1) * D_MODEL)
        hg = proj(OFF_GATE + br * D_MODEL, D_MODEL) + bg_ref[:, sl]
        gate_ref[:, sl] = jax.nn.sigmoid(hg).astype(BF16)


def _inproj(x2d, w_in, b_gate, cos_t, sin_t, seq):
    T = x2d.shape[0]
    tm = ROW_TILE
    tiles_per_seq = seq // tm
    row = lambda w: pl.BlockSpec((tm, w), lambda i: (i, 0))
    tab = pl.BlockSpec((tm, LANES), lambda i: (i % tiles_per_seq, 0))
    widths = (SB_WIDTH, SB_WIDTH, SB_WIDTH, RET_QK_WIDTH, RET_QK_WIDTH,
              RET_V_WIDTH, RET_V_WIDTH, N_BRANCHES * D_MODEL)
    return pl.pallas_call(
        _inproj_kernel,
        grid=(T // tm,),
        in_specs=[row(D_MODEL), _resident((D_MODEL, IN_WIDTH)),
                  _resident((1, N_BRANCHES * D_MODEL)), tab, tab],
        out_specs=[row(w) for w in widths],
        out_shape=[jax.ShapeDtypeStruct((T, w), BF16) for w in widths],
        compiler_params=_params("parallel"),
        name="inproj",
    )(x2d, w_in, b_gate, cos_t, sin_t)


def _sb_kernel(q_ref, k_ref, v_ref, cum_ref, o_ref):
    blk = pl.program_id(2)
    t = SB_TILE
    lane = lax.broadcasted_iota(jnp.int32, (t, LANES), 1)
    first = lane < SB_HEAD_DIM
    cum = cum_ref[...]
    row = lax.broadcasted_iota(jnp.int32, (t, 2 * t), 0)
    col = lax.broadcasted_iota(jnp.int32, (t, 2 * t), 1)
    causal = jnp.where(col >= t, col - t, col) < row

    def block_diag(x2):
        zero = jnp.zeros_like(x2)
        return jnp.concatenate([jnp.where(first, x2, zero), jnp.where(first, zero, x2)], axis=0)

    def kv(j):
        start = pl.multiple_of(j * t, t)
        return block_diag(k_ref[0, pl.ds(start, t), :]), block_diag(v_ref[0, pl.ds(start, t), :])

    def tile(q2, kbd, vbd, run, acc, mask):
        z = lax.dot_general(q2, kbd, (((1,), (1,)), ((), ())), preferred_element_type=F32)
        log_rem = -(jnp.maximum(z, 0.0) + jnp.log(1.0 + jnp.exp(-jnp.abs(z))))
        log_beta = z + log_rem
        if mask is not None:
            log_rem = jnp.where(mask, log_rem, 0.0)
        hi = log_rem.astype(BF16)
        lo = (log_rem - hi.astype(F32)).astype(BF16)
        sums = [jnp.dot(jnp.concatenate([hi[:, h * t:(h + 1) * t], lo[:, h * t:(h + 1) * t]], axis=1),
                        cum, preferred_element_type=F32) for h in range(2)]
        later = jnp.concatenate([sums[0][:, :t], sums[1][:, :t]], axis=1)
        rowsum = jnp.concatenate([sums[0][:, t:], sums[1][:, t:]], axis=1)
        w = jnp.exp(log_beta + later + run)
        if mask is not None:
            w = jnp.where(mask, w, 0.0)
        acc = acc + jnp.dot(w.astype(BF16), vbd, preferred_element_type=F32)
        return run + rowsum, acc

    def q_tile(u):
        return q_ref[0, u * t:(u + 1) * t, :]

    state = []
    for u in range(SB_QTILES):
        kbd, vbd = kv(blk * SB_QTILES + u)
        state.extend(tile(q_tile(u), kbd, vbd, jnp.zeros((t, 2 * t), F32),
                          jnp.zeros((t, LANES), F32), causal))

    def live(carry):
        return carry[1] > SB_UNDERFLOW

    def body(carry):
        s = carry[0]
        st = list(carry[2:])
        for step in range(SB_STEPS_PER_CHECK):
            for u in range(SB_QTILES):
                j = blk * SB_QTILES + u - (s + step)
                gone = jnp.where(j >= 0, 0.0, SB_GONE).astype(F32)
                kbd, vbd = kv(jnp.maximum(j, 0))
                st[2 * u], st[2 * u + 1] = tile(q_tile(u), kbd, vbd, st[2 * u] + gone,
                                                st[2 * u + 1], None)
        top = st[0]
        for u in range(1, SB_QTILES):
            top = jnp.maximum(top, st[2 * u])
        return (s + SB_STEPS_PER_CHECK, jnp.max(top), *st)

    out = lax.while_loop(live, body, (jnp.int32(1), jnp.float32(0.0), *state))
    for u in range(SB_QTILES):
        o_ref[0, u * t:(u + 1) * t, :] = out[3 + 2 * u].astype(BF16)


def _sb_attention(q, k, v, cum):
    B, S, _ = q.shape
    rows = SB_TILE * SB_QTILES
    pairs = SB_WIDTH // LANES
    qspec = pl.BlockSpec((1, rows, LANES), lambda b, p, i: (b, i, p))
    kvspec = pl.BlockSpec((1, S, LANES), lambda b, p, i: (b, 0, p))
    return pl.pallas_call(
        _sb_kernel,
        grid=(B, pairs, S // rows),
        in_specs=[qspec, kvspec, kvspec, _resident((2 * SB_TILE, 2 * SB_TILE))],
        out_specs=qspec,
        out_shape=jax.ShapeDtypeStruct((B, S, SB_WIDTH), BF16),
        compiler_params=_params("parallel", "parallel", "arbitrary"),
        name="sb_attn",
    )(q, k, v, cum)


def _ret_kernel(q_ref, k_ref, v_ref, g_ref, decay_ref, qd_ref, kd_ref, cd_ref, o_ref, state_ref):
    @pl.when(pl.program_id(2) == 0)
    def _():
        state_ref[...] = jnp.zeros_like(state_ref)

    c = RET_CHUNK
    for ci in range(RET_ROWS // c):
        sl = slice(ci * c, (ci + 1) * c)
        q = q_ref[0, sl, :]
        k = k_ref[0, sl, :]
        v = v_ref[0, sl, :]
        s = lax.dot_general(q, k, (((1,), (1,)), ((), ())), preferred_element_type=F32) * decay_ref[0]
        inner = jnp.dot(s.astype(BF16), v, preferred_element_type=F32)
        st = state_ref[...]
        cross = jnp.dot(q, st.astype(BF16), preferred_element_type=F32) * qd_ref[0]
        o = inner + cross
        vdec = (v.astype(F32) * kd_ref[0]).astype(BF16)
        state_ref[...] = st * cd_ref[0] + lax.dot_general(
            k, vdec, (((0,), (0,)), ((), ())), preferred_element_type=F32)
        mu = jnp.mean(o, axis=-1, keepdims=True)
        oc = o - mu
        var = jnp.mean(oc * oc, axis=-1, keepdims=True)
        normed = oc * lax.rsqrt(var + LN_EPS)
        o_ref[0, sl, :] = (g_ref[0, sl, :].astype(F32) * normed).astype(BF16)


def _retention(rq, rk, rv, rg, tables):
    B, S, _ = rq.shape
    c = RET_CHUNK
    qk = pl.BlockSpec((1, RET_ROWS, RET_QK_DIM), lambda b, h, r: (b, r, h))
    vv = pl.BlockSpec((1, RET_ROWS, RET_V_DIM), lambda b, h, r: (b, r, h))
    head = lambda rows, cols: pl.BlockSpec((1, rows, cols), lambda b, h, r: (h, 0, 0))
    return pl.pallas_call(
        _ret_kernel,
        grid=(B, RET_HEADS, S // RET_ROWS),
        in_specs=[qk, qk, vv, vv, head(c, c), head(c, RET_V_DIM), head(c, RET_V_DIM),
                  head(RET_QK_DIM, RET_V_DIM)],
        out_specs=vv,
        out_shape=jax.ShapeDtypeStruct((B, S, RET_V_WIDTH), BF16),
        scratch_shapes=[pltpu.VMEM((RET_QK_DIM, RET_V_DIM), F32)],
        compiler_params=_params("parallel", "parallel", "arbitrary"),
        name="retention",
    )(rq, rk, rv, rg, *tables)


def _retention_tables():
    c = RET_CHUNK
    log_gamma = jnp.log1p(-jnp.exp2(-5.0 - jnp.arange(RET_HEADS, dtype=F32)))
    idx = jnp.arange(c, dtype=F32)
    rel = idx[:, None] - idx[None, :]
    lg = log_gamma[:, None, None]
    decay = jnp.where(rel[None] >= 0, jnp.exp(lg * jnp.maximum(rel, 0.0)[None]), 0.0)
    qd = jnp.broadcast_to(jnp.exp(lg * (idx + 1.0)[None, :, None]), (RET_HEADS, c, RET_V_DIM))
    kd = jnp.broadcast_to(jnp.exp(lg * (c - 1.0 - idx)[None, :, None]), (RET_HEADS, c, RET_V_DIM))
    cd = jnp.broadcast_to(jnp.exp(lg * float(c)), (RET_HEADS, RET_QK_DIM, RET_V_DIM))
    return decay, qd, kd, cd


def _mix_kernel(a_ref, r_ref, gate_ref, x_ref, wsb_ref, wret_ref, wmix_ref, g_ref, b_ref, o_ref):
    y_sb = jnp.dot(a_ref[...], wsb_ref[...], preferred_element_type=F32)
    y_ret = jnp.dot(r_ref[...], wret_ref[...], preferred_element_type=F32)
    merged = (gate_ref[:, :D_MODEL].astype(F32) * y_sb
              + gate_ref[:, D_MODEL:].astype(F32) * y_ret)
    mix = jnp.dot(merged.astype(BF16), wmix_ref[...], preferred_element_type=F32)
    o_ref[...] = _layer_norm(DN_ALPHA * x_ref[...] + mix, g_ref[...], b_ref[...])


def _mix_ln1(attn, retg, gates, x2d, w_sb_o, w_ret_o, w_mix_o, g, b):
    T = x2d.shape[0]
    tm = ROW_TILE
    row = lambda w: pl.BlockSpec((tm, w), lambda i: (i, 0))
    return pl.pallas_call(
        _mix_kernel,
        grid=(T // tm,),
        in_specs=[row(SB_WIDTH), row(RET_V_WIDTH), row(N_BRANCHES * D_MODEL), row(D_MODEL),
                  _resident(w_sb_o.shape), _resident(w_ret_o.shape), _resident(w_mix_o.shape),
                  _resident(g.shape), _resident(b.shape)],
        out_specs=row(D_MODEL),
        out_shape=jax.ShapeDtypeStruct((T, D_MODEL), F32),
        compiler_params=_params("parallel"),
        name="mix_ln1",
    )(attn, retg, gates, x2d, w_sb_o, w_ret_o, w_mix_o, g, b)


def _memkv_kernel(m_ref, w_ref, k_ref, v_ref):
    mb = m_ref[0].astype(BF16)
    k_ref[0] = jnp.dot(mb, w_ref[:, :D_MODEL], preferred_element_type=F32).astype(BF16)
    v_ref[0] = jnp.dot(mb, w_ref[:, D_MODEL:], preferred_element_type=F32).astype(BF16)


def _mem_kv(mem, w_kv):
    B, M, _ = mem.shape
    blk = pl.BlockSpec((1, M, D_MODEL), lambda b: (b, 0, 0))
    return pl.pallas_call(
        _memkv_kernel,
        grid=(B,),
        in_specs=[blk, _resident(w_kv.shape)],
        out_specs=[blk, blk],
        out_shape=[jax.ShapeDtypeStruct((B, M, D_MODEL), BF16)] * 2,
        compiler_params=_params("parallel"),
        name="mem_kv",
    )(mem, w_kv)


def _xattn_kernel(x_ref, k_ref, v_ref, wq_ref, wo_ref, g_ref, b_ref, o_ref, ctx_ref):
    x = x_ref[0]
    q = (jnp.dot(x.astype(BF16), wq_ref[...], preferred_element_type=F32)
         * (MEM_HEAD_DIM ** -0.5)).astype(BF16)
    for h in range(MEM_HEADS):
        sl = slice(h * MEM_HEAD_DIM, (h + 1) * MEM_HEAD_DIM)
        s = lax.dot_general(q[:, sl], k_ref[0, :, sl], (((1,), (1,)), ((), ())),
                            preferred_element_type=F32)
        p = jnp.exp(s - jnp.max(s, axis=-1, keepdims=True))
        denom = jnp.sum(p, axis=-1, keepdims=True)
        ctx = jnp.dot(p.astype(BF16), v_ref[0, :, sl], preferred_element_type=F32)
        ctx_ref[:, sl] = (ctx / denom).astype(BF16)
    xa = jnp.dot(ctx_ref[...], wo_ref[...], preferred_element_type=F32)
    o_ref[0] = _layer_norm(DN_ALPHA * x + xa, g_ref[...], b_ref[...])


def _xattn_ln2(x3d, mk, mv, w_q, w_o, g, b):
    B, S, _ = x3d.shape
    M = mk.shape[1]
    tm = ROW_TILE
    row = pl.BlockSpec((1, tm, D_MODEL), lambda bi, i: (bi, i, 0))
    kv = pl.BlockSpec((1, M, D_MODEL), lambda bi, i: (bi, 0, 0))
    return pl.pallas_call(
        _xattn_kernel,
        grid=(B, S // tm),
        in_specs=[row, kv, kv, _resident(w_q.shape), _resident(w_o.shape),
                  _resident(g.shape), _resident(b.shape)],
        out_specs=row,
        out_shape=jax.ShapeDtypeStruct((B, S, D_MODEL), F32),
        scratch_shapes=[pltpu.VMEM((tm, D_MODEL), BF16)],
        compiler_params=_params("parallel", "arbitrary"),
        name="xattn_ln2",
    )(x3d, mk, mv, w_q, w_o, g, b)


def _ffn_kernel(x_ref, win_ref, wout_ref, g_ref, b_ref, o_ref, hid_ref):
    x = x_ref[...]
    xb = x.astype(BF16)
    for c in range(FFN_HIDDEN // FFN_CHUNK):
        lo = c * FFN_CHUNK
        a = jnp.dot(xb, win_ref[:, lo:lo + FFN_CHUNK], preferred_element_type=F32)
        gte = jnp.dot(xb, win_ref[:, FFN_HIDDEN + lo:FFN_HIDDEN + lo + FFN_CHUNK],
                      preferred_element_type=F32)
        hid_ref[:, lo:lo + FFN_CHUNK] = (a * jax.nn.sigmoid(a) * gte).astype(BF16)
    ff = jnp.dot(hid_ref[...], wout_ref[...], preferred_element_type=F32)
    o_ref[...] = _layer_norm(DN_ALPHA * x + ff, g_ref[...], b_ref[...])


def _ffn_ln3(x2d, w_in, w_out, g, b):
    T = x2d.shape[0]
    tm = ROW_TILE
    row = pl.BlockSpec((tm, D_MODEL), lambda i: (i, 0))
    return pl.pallas_call(
        _ffn_kernel,
        grid=(T // tm,),
        in_specs=[row, _resident(w_in.shape), _resident(w_out.shape),
                  _resident(g.shape), _resident(b.shape)],
        out_specs=row,
        out_shape=jax.ShapeDtypeStruct((T, D_MODEL), F32),
        scratch_shapes=[pltpu.VMEM((tm, FFN_HIDDEN), BF16)],
        compiler_params=_params("parallel"),
        name="ffn_ln3",
    )(x2d, w_in, w_out, g, b)


def _rope_tables(seq):
    half = RET_QK_DIM // 2
    inv = 1.0 / (ROPE_BASE ** (jnp.arange(half, dtype=F32) / half))
    ang = jnp.arange(seq, dtype=F32)[:, None] * inv[None, :]
    cos = jnp.cos(ang)
    sin = jnp.sin(ang)
    return jnp.concatenate([cos, cos], axis=-1), jnp.concatenate([-sin, sin], axis=-1)


def _cumsum_matrix():
    t = SB_TILE
    r = jnp.arange(t)
    upper = (r[:, None] > r[None, :]).astype(BF16)
    half = jnp.concatenate([upper, jnp.ones((t, t), BF16)], axis=1)
    return jnp.concatenate([half, half], axis=0)


def kernel(x, mem, w_in, b_gate, w_sb_o, w_ret_o, w_mix_o, ln1_g, ln1_b, w_mem_q, w_mem_kv,
           w_mem_o, ln2_g, ln2_b, w_ffn_in, w_ffn_out, ln3_g, ln3_b):
    B, S, D = x.shape
    assert D == D_MODEL and w_in.shape == (DEPTH, D_MODEL, IN_WIDTH)
    assert S % RET_ROWS == 0 and (B * S) % ROW_TILE == 0 and S % ROW_TILE == 0
    cos_t, sin_t = _rope_tables(S)
    cum = _cumsum_matrix()
    ret_tables = _retention_tables()
    x2d = x.reshape(B * S, D)
    for l in range(DEPTH):
        bf = lambda w: w[l].astype(BF16)
        vec = lambda p: p[l][None, :]
        sbq, sbk, sbv, rq, rk, rv, rg, gates = _inproj(
            x2d, bf(w_in), vec(b_gate), cos_t, sin_t, S)
        seq3 = lambda a: a.reshape(B, S, a.shape[-1])
        attn = _sb_attention(seq3(sbq), seq3(sbk), seq3(sbv), cum)
        retg = _retention(seq3(rq), seq3(rk), seq3(rv), seq3(rg), ret_tables)
        x1 = _mix_ln1(attn.reshape(B * S, SB_WIDTH), retg.reshape(B * S, RET_V_WIDTH), gates, x2d,
                      bf(w_sb_o), bf(w_ret_o), bf(w_mix_o), vec(ln1_g), vec(ln1_b))
        mk, mv = _mem_kv(mem, bf(w_mem_kv))
        x2 = _xattn_ln2(x1.reshape(B, S, D), mk, mv, bf(w_mem_q), bf(w_mem_o),
                        vec(ln2_g), vec(ln2_b))
        x2d = _ffn_ln3(x2.reshape(B * S, D), bf(w_ffn_in), bf(w_ffn_out), vec(ln3_g), vec(ln3_b))
    return x2d.reshape(B, S, D)
```

```python
import functools

import jax
import jax.numpy as jnp
from jax import lax
from jax.experimental import pallas as pl
from jax.experimental.pallas import tpu as pltpu

F32 = jnp.float32
BF16 = jnp.bfloat16

D_MODEL = 1024
DEPTH = 1
SB_HEADS = 8
SB_HEAD_DIM = 64
SB_WIDTH = SB_HEADS * SB_HEAD_DIM
RET_HEADS = 4
RET_QK_DIM = 128
RET_V_DIM = 256
RET_QK_WIDTH = RET_HEADS * RET_QK_DIM
RET_V_WIDTH = RET_HEADS * RET_V_DIM
ROPE_BASE = 10000.0
N_BRANCHES = 2
OFF_SB_Q = 0
OFF_SB_K = OFF_SB_Q + SB_WIDTH
OFF_SB_V = OFF_SB_K + SB_WIDTH
OFF_RET_Q = OFF_SB_V + SB_WIDTH
OFF_RET_K = OFF_RET_Q + RET_QK_WIDTH
OFF_RET_V = OFF_RET_K + RET_QK_WIDTH
OFF_RET_G = OFF_RET_V + RET_V_WIDTH
OFF_GATE = OFF_RET_G + RET_V_WIDTH
IN_WIDTH = OFF_GATE + N_BRANCHES * D_MODEL
MEM_HEADS = 4
MEM_HEAD_DIM = D_MODEL // MEM_HEADS
FFN_HIDDEN = 2816
DN_ALPHA = (2.0 * DEPTH) ** 0.25
LN_EPS = 1e-5

LANES = 128
VMEM_LIMIT_BYTES = 56 * 1024 * 1024
ROW_TILE = 512
SB_TILE = 128
SB_QTILES = 4
SB_FIRST_STEPS = 3
SB_STEPS_PER_CHECK = 2
SB_UNDERFLOW = -150.5
LOG2_E = 1.4426950408889634
RET_CHUNK = 256
RET_ROWS = 1024
FFN_CHUNK = 256


def _resident(shape):
    zeros = (0,) * len(shape)
    return pl.BlockSpec(shape, lambda *_: zeros, pipeline_mode=pl.Buffered(1))


def _params(*sem):
    return pltpu.CompilerParams(dimension_semantics=sem, vmem_limit_bytes=VMEM_LIMIT_BYTES)


def _layer_norm(y, g, b):
    mu = jnp.mean(y, axis=-1, keepdims=True)
    yc = y - mu
    var = jnp.mean(yc * yc, axis=-1, keepdims=True)
    return yc * lax.rsqrt(var + LN_EPS) * g + b


def _inproj_kernel(x_ref, w_ref, bg_ref, cos_ref, sin_ref,
                   sbq_ref, sbk_ref, sbv_ref, rq_ref, rk_ref, rv_ref, rg_ref, gate_ref):
    xb = x_ref[...].astype(BF16)

    def proj(off, width):
        return jnp.dot(xb, w_ref[:, off:off + width], preferred_element_type=F32)

    sbq_ref[...] = (proj(OFF_SB_Q, SB_WIDTH) * (SB_HEAD_DIM ** -0.5 * LOG2_E)).astype(BF16)

    def store_head_split(h, out_ref):
        hb = h.astype(BF16)
        first = lax.broadcasted_iota(jnp.int32, (hb.shape[0], LANES), 1) < SB_HEAD_DIM
        zero = jnp.zeros((hb.shape[0], LANES), BF16)
        for p in range(SB_WIDTH // LANES):
            pair = hb[:, p * LANES:(p + 1) * LANES]
            out_ref[:, 2 * p * LANES:(2 * p + 1) * LANES] = jnp.where(first, pair, zero)
            out_ref[:, (2 * p + 1) * LANES:(2 * p + 2) * LANES] = jnp.where(first, zero, pair)

    store_head_split(proj(OFF_SB_K, SB_WIDTH), sbk_ref)
    store_head_split(proj(OFF_SB_V, SB_WIDTH), sbv_ref)

    cos = cos_ref[...]
    sin = sin_ref[...]

    def rope_store(h, out_ref, scale):
        for hd in range(RET_HEADS):
            sl = slice(hd * RET_QK_DIM, (hd + 1) * RET_QK_DIM)
            xh = h[:, sl]
            rot = xh * cos + pltpu.roll(xh, RET_QK_DIM // 2, 1) * sin
            if scale != 1.0:
                rot = rot * scale
            out_ref[:, sl] = rot.astype(BF16)

    rope_store(proj(OFF_RET_Q, RET_QK_WIDTH), rq_ref, RET_QK_DIM ** -0.5)
    rope_store(proj(OFF_RET_K, RET_QK_WIDTH), rk_ref, 1.0)
    rv_ref[...] = proj(OFF_RET_V, RET_V_WIDTH).astype(BF16)
    g = proj(OFF_RET_G, RET_V_WIDTH)
    rg_ref[...] = (g * jax.nn.sigmoid(g)).astype(BF16)
    for br in range(N_BRANCHES):
        sl = slice(br * D_MODEL, (br + 1) * D_MODEL)
        hg = proj(OFF_GATE + br * D_MODEL, D_MODEL) + bg_ref[:, sl]
        gate_ref[:, sl] = jax.nn.sigmoid(hg).astype(BF16)


def _inproj(x2d, w_in, b_gate, cos_t, sin_t, seq):
    T = x2d.shape[0]
    tm = ROW_TILE
    tiles_per_seq = seq // tm
    row = lambda w: pl.BlockSpec((tm, w), lambda i: (i, 0))
    tab = pl.BlockSpec((tm, LANES), lambda i: (i % tiles_per_seq, 0))
    widths = (SB_WIDTH, 2 * SB_WIDTH, 2 * SB_WIDTH, RET_QK_WIDTH, RET_QK_WIDTH,
              RET_V_WIDTH, RET_V_WIDTH, N_BRANCHES * D_MODEL)
    return pl.pallas_call(
        _inproj_kernel,
        grid=(T // tm,),
        in_specs=[row(D_MODEL), _resident((D_MODEL, IN_WIDTH)),
                  _resident((1, N_BRANCHES * D_MODEL)), tab, tab],
        out_specs=[row(w) for w in widths],
        out_shape=[jax.ShapeDtypeStruct((T, w), BF16) for w in widths],
        compiler_params=_params("parallel"),
        name="inproj",
    )(x2d, w_in, b_gate, cos_t, sin_t)


def _sb_kernel(q_ref, k_ref, v_ref, cum_ref, o_ref):
    blk = pl.program_id(2)
    t = SB_TILE
    cum = cum_ref[...]
    row = lax.broadcasted_iota(jnp.int32, (t, 2 * t), 0)
    col = lax.broadcasted_iota(jnp.int32, (t, 2 * t), 1)
    causal = jnp.where(col >= t, col - t, col) < row

    def stack(x):
        return jnp.concatenate([x[:, :LANES], x[:, LANES:]], axis=0)

    def kv(j):
        start = pl.multiple_of(j * t, t)
        return stack(k_ref[0, pl.ds(start, t), :]), stack(v_ref[0, pl.ds(start, t), :])

    def q_tile(u):
        return q_ref[0, u * t:(u + 1) * t, :]

    def scores(q_rows, kbd):
        return lax.dot_general(q_rows, kbd, (((1,), (1,)), ((), ())), preferred_element_type=F32)

    def logs(y, mask):
        log_beta = jnp.minimum(y, 0.0) - jnp.log2(1.0 + jnp.exp2(-jnp.abs(y)))
        log_rem = log_beta - y
        if mask is not None:
            log_rem = jnp.where(mask, log_rem, 0.0)
        hi = log_rem.astype(BF16)
        lo = (log_rem - hi.astype(F32)).astype(BF16)
        rows = [jnp.concatenate([hi[:, h * t:(h + 1) * t], lo[:, h * t:(h + 1) * t]], axis=1)
                for h in range(2)]
        return log_beta, jnp.concatenate(rows, axis=0)

    def suffix_sums(operands):
        sums = jnp.dot(jnp.concatenate(operands, axis=0), cum, preferred_element_type=F32)
        out = []
        for n in range(len(operands)):
            a = sums[2 * n * t:(2 * n + 1) * t]
            b = sums[(2 * n + 1) * t:(2 * n + 2) * t]
            out.append((jnp.concatenate([a[:, :t], b[:, :t]], axis=1),
                        jnp.concatenate([a[:, t:], b[:, t:]], axis=1)))
        return out

    base = blk * SB_QTILES

    def walk(s, nsteps, run, acc):
        from_diag = isinstance(s, int) and s == 0
        steps = range(nsteps)
        offsets = range(1 - nsteps, SB_QTILES)
        members = {d: [(u, st) for st in steps for u in range(SB_QTILES) if u - st == d]
                   for d in offsets}
        order = [(u, st) for st in steps for u in range(SB_QTILES)]
        is_diag = lambda st: from_diag and st == 0
        y, values = {}, {}
        for d in offsets:
            j = base + d - s
            kbd, vbd = kv(jnp.maximum(j, 0))
            if not (from_diag and d >= 0):
                vbd = jnp.where(j >= 0, vbd, jnp.zeros_like(vbd))
            values[d] = vbd
            stacked = scores(jnp.concatenate([q_tile(u) for u, _ in members[d]], axis=0), kbd)
            for n, key in enumerate(members[d]):
                y[key] = stacked[n * t:(n + 1) * t]
        parts = {key: logs(y[key], causal if is_diag(key[1]) else None) for key in order}
        sums = []
        for st in steps:
            sums.extend(suffix_sums([parts[key][1] for key in order if key[1] == st]))
        run, acc, w = list(run), list(acc), {}
        for key, (later, rowsum) in zip(order, sums):
            u = key[0]
            if is_diag(key[1]):
                w[key] = jnp.where(causal, jnp.exp2(parts[key][0] + later), 0.0).astype(BF16)
                run[u] = rowsum
            else:
                w[key] = jnp.exp2(parts[key][0] + later + run[u]).astype(BF16)
                run[u] = run[u] + rowsum
        for d in offsets:
            ctx = jnp.dot(jnp.concatenate([w[key] for key in members[d]], axis=0), values[d],
                          preferred_element_type=F32)
            for n, (u, _) in enumerate(members[d]):
                part = ctx[n * t:(n + 1) * t]
                acc[u] = part if acc[u] is None else acc[u] + part
        top = run[0]
        for u in range(1, SB_QTILES):
            top = jnp.maximum(top, run[u])
        return jnp.max(top), run, acc

    top, run, acc = walk(0, SB_FIRST_STEPS, [None] * SB_QTILES, [None] * SB_QTILES)

    def live(carry):
        return jnp.logical_and(carry[1] > SB_UNDERFLOW, carry[0] <= base + SB_QTILES - 1)

    def body(carry):
        s = carry[0]
        top, run, acc = walk(s, SB_STEPS_PER_CHECK, carry[2:2 + SB_QTILES], carry[2 + SB_QTILES:])
        return (s + SB_STEPS_PER_CHECK, top, *run, *acc)

    out = lax.while_loop(live, body, (jnp.int32(SB_FIRST_STEPS), top, *run, *acc))
    for u in range(SB_QTILES):
        o_ref[0, u * t:(u + 1) * t, :] = out[2 + SB_QTILES + u].astype(BF16)


def _sb_attention(q, k, v, cum):
    B, S, _ = q.shape
    rows = SB_TILE * SB_QTILES
    pairs = SB_WIDTH // LANES
    qspec = pl.BlockSpec((1, rows, LANES), lambda b, p, i: (b, i, p))
    kvspec = pl.BlockSpec((1, S, 2 * LANES), lambda b, p, i: (b, 0, p))
    return pl.pallas_call(
        _sb_kernel,
        grid=(B, pairs, S // rows),
        in_specs=[qspec, kvspec, kvspec, _resident((2 * SB_TILE, 2 * SB_TILE))],
        out_specs=qspec,
        out_shape=jax.ShapeDtypeStruct((B, S, SB_WIDTH), BF16),
        compiler_params=_params("parallel", "parallel", "arbitrary"),
        name="sb_attn",
    )(q, k, v, cum)


def _ret_kernel(q_ref, k_ref, v_ref, g_ref, decay_ref, qd_ref, kd_ref, cd_ref, o_ref, state_ref):
    @pl.when(pl.program_id(2) == 0)
    def _():
        state_ref[...] = jnp.zeros_like(state_ref)

    c = RET_CHUNK
    for ci in range(RET_ROWS // c):
        sl = slice(ci * c, (ci + 1) * c)
        q = q_ref[0, sl, :]
        k = k_ref[0, sl, :]
        v = v_ref[0, sl, :]
        s = lax.dot_general(q, k, (((1,), (1,)), ((), ())), preferred_element_type=F32) * decay_ref[0]
        inner = jnp.dot(s.astype(BF16), v, preferred_element_type=F32)
        st = state_ref[...]
        cross = jnp.dot(q, st.astype(BF16), preferred_element_type=F32) * qd_ref[0]
        o = inner + cross
        vdec = (v.astype(F32) * kd_ref[0]).astype(BF16)
        state_ref[...] = st * cd_ref[0] + lax.dot_general(
            k, vdec, (((0,), (0,)), ((), ())), preferred_element_type=F32)
        mu = jnp.mean(o, axis=-1, keepdims=True)
        oc = o - mu
        var = jnp.mean(oc * oc, axis=-1, keepdims=True)
        normed = oc * lax.rsqrt(var + LN_EPS)
        o_ref[0, sl, :] = (g_ref[0, sl, :].astype(F32) * normed).astype(BF16)


def _retention(rq, rk, rv, rg, tables):
    B, S, _ = rq.shape
    c = RET_CHUNK
    qk = pl.BlockSpec((1, RET_ROWS, RET_QK_DIM), lambda b, h, r: (b, r, h))
    vv = pl.BlockSpec((1, RET_ROWS, RET_V_DIM), lambda b, h, r: (b, r, h))
    head = lambda rows, cols: pl.BlockSpec((1, rows, cols), lambda b, h, r: (h, 0, 0))
    return pl.pallas_call(
        _ret_kernel,
        grid=(B, RET_HEADS, S // RET_ROWS),
        in_specs=[qk, qk, vv, vv, head(c, c), head(c, RET_V_DIM), head(c, RET_V_DIM),
                  head(RET_QK_DIM, RET_V_DIM)],
        out_specs=vv,
        out_shape=jax.ShapeDtypeStruct((B, S, RET_V_WIDTH), BF16),
        scratch_shapes=[pltpu.VMEM((RET_QK_DIM, RET_V_DIM), F32)],
        compiler_params=_params("parallel", "parallel", "arbitrary"),
        name="retention",
    )(rq, rk, rv, rg, *tables)


def _retention_tables():
    c = RET_CHUNK
    log_gamma = jnp.log1p(-jnp.exp2(-5.0 - jnp.arange(RET_HEADS, dtype=F32)))
    idx = jnp.arange(c, dtype=F32)
    rel = idx[:, None] - idx[None, :]
    lg = log_gamma[:, None, None]
    decay = jnp.where(rel[None] >= 0, jnp.exp(lg * jnp.maximum(rel, 0.0)[None]), 0.0)
    qd = jnp.broadcast_to(jnp.exp(lg * (idx + 1.0)[None, :, None]), (RET_HEADS, c, RET_V_DIM))
    kd = jnp.broadcast_to(jnp.exp(lg * (c - 1.0 - idx)[None, :, None]), (RET_HEADS, c, RET_V_DIM))
    cd = jnp.broadcast_to(jnp.exp(lg * float(c)), (RET_HEADS, RET_QK_DIM, RET_V_DIM))
    return decay, qd, kd, cd


def _mix_kernel(a_ref, r_ref, gate_ref, x_ref, wsb_ref, wret_ref, wmix_ref, g_ref, b_ref, o_ref):
    y_sb = jnp.dot(a_ref[...], wsb_ref[...], preferred_element_type=F32)
    y_ret = jnp.dot(r_ref[...], wret_ref[...], preferred_element_type=F32)
    merged = (gate_ref[:, :D_MODEL].astype(F32) * y_sb
              + gate_ref[:, D_MODEL:].astype(F32) * y_ret)
    mix = jnp.dot(merged.astype(BF16), wmix_ref[...], preferred_element_type=F32)
    o_ref[...] = _layer_norm(DN_ALPHA * x_ref[...] + mix, g_ref[...], b_ref[...])


def _mix_ln1(attn, retg, gates, x2d, w_sb_o, w_ret_o, w_mix_o, g, b):
    T = x2d.shape[0]
    tm = ROW_TILE
    row = lambda w: pl.BlockSpec((tm, w), lambda i: (i, 0))
    return pl.pallas_call(
        _mix_kernel,
        grid=(T // tm,),
        in_specs=[row(SB_WIDTH), row(RET_V_WIDTH), row(N_BRANCHES * D_MODEL), row(D_MODEL),
                  _resident(w_sb_o.shape), _resident(w_ret_o.shape), _resident(w_mix_o.shape),
                  _resident(g.shape), _resident(b.shape)],
        out_specs=row(D_MODEL),
        out_shape=jax.ShapeDtypeStruct((T, D_MODEL), F32),
        compiler_params=_params("parallel"),
        name="mix_ln1",
    )(attn, retg, gates, x2d, w_sb_o, w_ret_o, w_mix_o, g, b)


def _memkv_kernel(m_ref, w_ref, k_ref, v_ref):
    mb = m_ref[0].astype(BF16)
    k_ref[0] = jnp.dot(mb, w_ref[:, :D_MODEL], preferred_element_type=F32).astype(BF16)
    v_ref[0] = jnp.dot(mb, w_ref[:, D_MODEL:], preferred_element_type=F32).astype(BF16)


def _mem_kv(mem, w_kv):
    B, M, _ = mem.shape
    blk = pl.BlockSpec((1, M, D_MODEL), lambda b: (b, 0, 0))
    return pl.pallas_call(
        _memkv_kernel,
        grid=(B,),
        in_specs=[blk, _resident(w_kv.shape)],
        out_specs=[blk, blk],
        out_shape=[jax.ShapeDtypeStruct((B, M, D_MODEL), BF16)] * 2,
        compiler_params=_params("parallel"),
        name="mem_kv",
    )(mem, w_kv)


def _xattn_kernel(x_ref, k_ref, v_ref, wq_ref, wo_ref, g_ref, b_ref, o_ref, ctx_ref):
    x = x_ref[0]
    q = (jnp.dot(x.astype(BF16), wq_ref[...], preferred_element_type=F32)
         * (MEM_HEAD_DIM ** -0.5)).astype(BF16)
    for h in range(MEM_HEADS):
        sl = slice(h * MEM_HEAD_DIM, (h + 1) * MEM_HEAD_DIM)
        s = lax.dot_general(q[:, sl], k_ref[0, :, sl], (((1,), (1,)), ((), ())),
                            preferred_element_type=F32)
        p = jnp.exp(s - jnp.max(s, axis=-1, keepdims=True))
        denom = jnp.sum(p, axis=-1, keepdims=True)
        ctx = jnp.dot(p.astype(BF16), v_ref[0, :, sl], preferred_element_type=F32)
        ctx_ref[:, sl] = (ctx / denom).astype(BF16)
    xa = jnp.dot(ctx_ref[...], wo_ref[...], preferred_element_type=F32)
    o_ref[0] = _layer_norm(DN_ALPHA * x + xa, g_ref[...], b_ref[...])


def _xattn_ln2(x3d, mk, mv, w_q, w_o, g, b):
    B, S, _ = x3d.shape
    M = mk.shape[1]
    tm = ROW_TILE
    row = pl.BlockSpec((1, tm, D_MODEL), lambda bi, i: (bi, i, 0))
    kv = pl.BlockSpec((1, M, D_MODEL), lambda bi, i: (bi, 0, 0))
    return pl.pallas_call(
        _xattn_kernel,
        grid=(B, S // tm),
        in_specs=[row, kv, kv, _resident(w_q.shape), _resident(w_o.shape),
                  _resident(g.shape), _resident(b.shape)],
        out_specs=row,
        out_shape=jax.ShapeDtypeStruct((B, S, D_MODEL), F32),
        scratch_shapes=[pltpu.VMEM((tm, D_MODEL), BF16)],
        compiler_params=_params("parallel", "arbitrary"),
        name="xattn_ln2",
    )(x3d, mk, mv, w_q, w_o, g, b)


def _ffn_kernel(x_ref, win_ref, wout_ref, g_ref, b_ref, o_ref, hid_ref):
    x = x_ref[...]
    xb = x.astype(BF16)
    for c in range(FFN_HIDDEN // FFN_CHUNK):
        lo = c * FFN_CHUNK
        a = jnp.dot(xb, win_ref[:, lo:lo + FFN_CHUNK], preferred_element_type=F32)
        gte = jnp.dot(xb, win_ref[:, FFN_HIDDEN + lo:FFN_HIDDEN + lo + FFN_CHUNK],
                      preferred_element_type=F32)
        hid_ref[:, lo:lo + FFN_CHUNK] = (a * jax.nn.sigmoid(a) * gte).astype(BF16)
    ff = jnp.dot(hid_ref[...], wout_ref[...], preferred_element_type=F32)
    o_ref[...] = _layer_norm(DN_ALPHA * x + ff, g_ref[...], b_ref[...])


def _ffn_ln3(x2d, w_in, w_out, g, b):
    T = x2d.shape[0]
    tm = ROW_TILE
    row = pl.BlockSpec((tm, D_MODEL), lambda i: (i, 0))
    return pl.pallas_call(
        _ffn_kernel,
        grid=(T // tm,),
        in_specs=[row, _resident(w_in.shape), _resident(w_out.shape),
                  _resident(g.shape), _resident(b.shape)],
        out_specs=row,
        out_shape=jax.ShapeDtypeStruct((T, D_MODEL), F32),
        scratch_shapes=[pltpu.VMEM((tm, FFN_HIDDEN), BF16)],
        compiler_params=_params("parallel"),
        name="ffn_ln3",
    )(x2d, w_in, w_out, g, b)


def _rope_tables(seq):
    half = RET_QK_DIM // 2
    inv = 1.0 / (ROPE_BASE ** (jnp.arange(half, dtype=F32) / half))
    ang = jnp.arange(seq, dtype=F32)[:, None] * inv[None, :]
    cos = jnp.cos(ang)
    sin = jnp.sin(ang)
    return jnp.concatenate([cos, cos], axis=-1), jnp.concatenate([-sin, sin], axis=-1)


def _cumsum_matrix():
    t = SB_TILE
    r = jnp.arange(t)
    upper = (r[:, None] > r[None, :]).astype(BF16)
    half = jnp.concatenate([upper, jnp.ones((t, t), BF16)], axis=1)
    return jnp.concatenate([half, half], axis=0)


def kernel(x, mem, w_in, b_gate, w_sb_o, w_ret_o, w_mix_o, ln1_g, ln1_b, w_mem_q, w_mem_kv,
           w_mem_o, ln2_g, ln2_b, w_ffn_in, w_ffn_out, ln3_g, ln3_b):
    B, S, D = x.shape
    assert D == D_MODEL and w_in.shape == (DEPTH, D_MODEL, IN_WIDTH)
    assert S % RET_ROWS == 0 and (B * S) % ROW_TILE == 0 and S % ROW_TILE == 0
    cos_t, sin_t = _rope_tables(S)
    cum = _cumsum_matrix()
    ret_tables = _retention_tables()
    x2d = x.reshape(B * S, D)
    for l in range(DEPTH):
        bf = lambda w: w[l].astype(BF16)
        vec = lambda p: p[l][None, :]
        sbq, sbk, sbv, rq, rk, rv, rg, gates = _inproj(
            x2d, bf(w_in), vec(b_gate), cos_t, sin_t, S)
        seq3 = lambda a: a.reshape(B, S, a.shape[-1])
        attn = _sb_attention(seq3(sbq), seq3(sbk), seq3(sbv), cum)
        retg = _retention(seq3(rq), seq3(rk), seq3(rv), seq3(rg), ret_tables)
        x1 = _mix_ln1(attn.reshape(B * S, SB_WIDTH), retg.reshape(B * S, RET_V_WIDTH), gates, x2d,
                      bf(w_sb_o), bf(w_ret_o), bf(w_mix_o), vec(ln1_g), vec(ln1_b))
        mk, mv = _mem_kv(mem, bf(w_mem_kv))
        x2 = _xattn_ln2(x1.reshape(B, S, D), mk, mv, bf(w_mem_q), bf(w_mem_o),
                        vec(ln2_g), vec(ln2_b))
        x2d = _ffn_ln3(x2.reshape(B * S, D), bf(w_ffn_in), bf(w_ffn_out), vec(ln3_g), vec(ln3_b))
    return x2d.reshape(B, S, D)
```

```python
import functools

import jax
import jax.numpy as jnp
from jax import lax
from jax.experimental import pallas as pl
from jax.experimental.pallas import tpu as pltpu

F32 = jnp.float32
BF16 = jnp.bfloat16

D_MODEL = 1024
DEPTH = 1
SB_HEADS = 8
SB_HEAD_DIM = 64
SB_WIDTH = SB_HEADS * SB_HEAD_DIM
RET_HEADS = 4
RET_QK_DIM = 128
RET_V_DIM = 256
RET_QK_WIDTH = RET_HEADS * RET_QK_DIM
RET_V_WIDTH = RET_HEADS * RET_V_DIM
ROPE_BASE = 10000.0
N_BRANCHES = 2
OFF_SB_Q = 0
OFF_SB_K = OFF_SB_Q + SB_WIDTH
OFF_SB_V = OFF_SB_K + SB_WIDTH
OFF_RET_Q = OFF_SB_V + SB_WIDTH
OFF_RET_K = OFF_RET_Q + RET_QK_WIDTH
OFF_RET_V = OFF_RET_K + RET_QK_WIDTH
OFF_RET_G = OFF_RET_V + RET_V_WIDTH
OFF_GATE = OFF_RET_G + RET_V_WIDTH
IN_WIDTH = OFF_GATE + N_BRANCHES * D_MODEL
MEM_HEADS = 4
MEM_HEAD_DIM = D_MODEL // MEM_HEADS
FFN_HIDDEN = 2816
DN_ALPHA = (2.0 * DEPTH) ** 0.25
LN_EPS = 1e-5

LANES = 128
VMEM_LIMIT_BYTES = 56 * 1024 * 1024
ROW_TILE = 512
SUB_ROWS = 256
DENSE_SUBTILES = 4
SB_TILE = 128
SB_QTILES = 4
SB_TOP_ROWS = 32
SB_FIRST_STEPS = 3
SB_STEPS_PER_CHECK = 2
SB_UNDERFLOW = -150.5
LOG2_E = 1.4426950408889634
RET_CHUNK = 256
RET_ROWS = 1024
FFN_CHUNK = 256


def _resident(shape):
    zeros = (0,) * len(shape)
    return pl.BlockSpec(shape, lambda *_: zeros, pipeline_mode=pl.Buffered(1))


def _params(*sem):
    return pltpu.CompilerParams(dimension_semantics=sem, vmem_limit_bytes=VMEM_LIMIT_BYTES)


def _layer_norm(y, g, b):
    mu = jnp.mean(y, axis=-1, keepdims=True)
    yc = y - mu
    var = jnp.mean(yc * yc, axis=-1, keepdims=True)
    return yc * lax.rsqrt(var + LN_EPS) * g + b


def _inproj_kernel(x_ref, w_ref, bg_ref, cos_ref, sin_ref,
                   sbq_ref, sbk_ref, sbv_ref, rq_ref, rk_ref, rv_ref, rg_ref, gate_ref):
    xb = x_ref[...].astype(BF16)

    def proj(off, width):
        return jnp.dot(xb, w_ref[:, off:off + width], preferred_element_type=F32)

    sbq_ref[...] = (proj(OFF_SB_Q, SB_WIDTH) * (SB_HEAD_DIM ** -0.5 * LOG2_E)).astype(BF16)

    def store_head_split(h, out_ref):
        hb = h.astype(BF16)
        first = lax.broadcasted_iota(jnp.int32, (hb.shape[0], LANES), 1) < SB_HEAD_DIM
        zero = jnp.zeros((hb.shape[0], LANES), BF16)
        for p in range(SB_WIDTH // LANES):
            pair = hb[:, p * LANES:(p + 1) * LANES]
            out_ref[:, 2 * p * LANES:(2 * p + 1) * LANES] = jnp.where(first, pair, zero)
            out_ref[:, (2 * p + 1) * LANES:(2 * p + 2) * LANES] = jnp.where(first, zero, pair)

    store_head_split(proj(OFF_SB_K, SB_WIDTH), sbk_ref)
    store_head_split(proj(OFF_SB_V, SB_WIDTH), sbv_ref)

    cos = cos_ref[...]
    sin = sin_ref[...]

    def rope_store(h, out_ref, scale):
        for hd in range(RET_HEADS):
            sl = slice(hd * RET_QK_DIM, (hd + 1) * RET_QK_DIM)
            xh = h[:, sl]
            rot = xh * cos + pltpu.roll(xh, RET_QK_DIM // 2, 1) * sin
            if scale != 1.0:
                rot = rot * scale
            out_ref[:, sl] = rot.astype(BF16)

    rope_store(proj(OFF_RET_Q, RET_QK_WIDTH), rq_ref, RET_QK_DIM ** -0.5)
    rope_store(proj(OFF_RET_K, RET_QK_WIDTH), rk_ref, 1.0)
    rv_ref[...] = proj(OFF_RET_V, RET_V_WIDTH).astype(BF16)
    g = proj(OFF_RET_G, RET_V_WIDTH)
    rg_ref[...] = (g * jax.nn.sigmoid(g)).astype(BF16)
    for br in range(N_BRANCHES):
        sl = slice(br * D_MODEL, (br + 1) * D_MODEL)
        hg = proj(OFF_GATE + br * D_MODEL, D_MODEL) + bg_ref[:, sl]
        gate_ref[:, sl] = jax.nn.sigmoid(hg).astype(BF16)


def _inproj(x2d, w_in, b_gate, cos_t, sin_t, seq):
    T = x2d.shape[0]
    tm = ROW_TILE
    tiles_per_seq = seq // tm
    row = lambda w: pl.BlockSpec((tm, w), lambda i: (i, 0))
    tab = pl.BlockSpec((tm, LANES), lambda i: (i % tiles_per_seq, 0))
    widths = (SB_WIDTH, 2 * SB_WIDTH, 2 * SB_WIDTH, RET_QK_WIDTH, RET_QK_WIDTH,
              RET_V_WIDTH, RET_V_WIDTH, N_BRANCHES * D_MODEL)
    return pl.pallas_call(
        _inproj_kernel,
        grid=(T // tm,),
        in_specs=[row(D_MODEL), _resident((D_MODEL, IN_WIDTH)),
                  _resident((1, N_BRANCHES * D_MODEL)), tab, tab],
        out_specs=[row(w) for w in widths],
        out_shape=[jax.ShapeDtypeStruct((T, w), BF16) for w in widths],
        compiler_params=_params("parallel"),
        name="inproj",
    )(x2d, w_in, b_gate, cos_t, sin_t)


def _sb_kernel(q_ref, k_ref, v_ref, cum_ref, cum3_ref, o_ref):
    blk = pl.program_id(2)
    t = SB_TILE
    cum = cum_ref[...]
    row = lax.broadcasted_iota(jnp.int32, (t, 2 * t), 0)
    col = lax.broadcasted_iota(jnp.int32, (t, 2 * t), 1)
    causal = jnp.where(col >= t, col - t, col) < row

    def stack(x):
        return jnp.concatenate([x[:, :LANES], x[:, LANES:]], axis=0)

    def kv(j):
        start = pl.multiple_of(j * t, t)
        return stack(k_ref[0, pl.ds(start, t), :]), stack(v_ref[0, pl.ds(start, t), :])

    def q_tile(u):
        return q_ref[0, u * t:(u + 1) * t, :]

    def scores(q_rows, kbd):
        return lax.dot_general(q_rows, kbd, (((1,), (1,)), ((), ())), preferred_element_type=F32)

    def logs(y, mask):
        log_beta = jnp.minimum(y, 0.0) - jnp.log2(1.0 + jnp.exp2(-jnp.abs(y)))
        log_rem = log_beta - y
        if mask is not None:
            log_rem = jnp.where(mask, log_rem, 0.0)
        hi = log_rem.astype(BF16)
        lo = (log_rem - hi.astype(F32)).astype(BF16)
        rows = [jnp.concatenate([hi[:, h * t:(h + 1) * t], lo[:, h * t:(h + 1) * t]], axis=1)
                for h in range(2)]
        return log_beta, jnp.concatenate(rows, axis=0)

    def suffix_sums(operands):
        sums = jnp.dot(jnp.concatenate(operands, axis=0), cum, preferred_element_type=F32)
        out = []
        for n in range(len(operands)):
            a = sums[2 * n * t:(2 * n + 1) * t]
            b = sums[(2 * n + 1) * t:(2 * n + 2) * t]
            out.append((jnp.concatenate([a[:, :t], b[:, :t]], axis=1),
                        jnp.concatenate([a[:, t:], b[:, t:]], axis=1)))
        return out

    base = blk * SB_QTILES

    def fast_logs(y, mask):
        log_beta = jnp.minimum(y, 0.0) - jnp.log2(1.0 + jnp.exp2(-jnp.abs(y)))
        log_rem = log_beta - y
        if mask is not None:
            log_rem = jnp.where(mask, log_rem, 0.0)
        return log_beta, log_rem.astype(BF16)

    def fast():
        top = SB_TOP_ROWS
        nrows = (t, t, top)
        offsets = range(1 - len(nrows), SB_QTILES)
        members = {d: [(d + st, st) for st in range(len(nrows)) if 0 <= d + st < SB_QTILES]
                   for d in offsets}
        y, values = {}, {}
        for d in offsets:
            j = base + d
            kbd, vbd = kv(jnp.maximum(j, 0))
            if d < 0:
                vbd = jnp.where(j >= 0, vbd, jnp.zeros_like(vbd))
            values[d] = vbd
            q_rows = [q_ref[0, u * t:u * t + nrows[st], :] for u, st in members[d]]
            stacked = scores(jnp.concatenate(q_rows, axis=0), kbd)
            off = 0
            for u, st in members[d]:
                y[(u, st)] = stacked[off:off + nrows[st]]
                off += nrows[st]
        lb, hi = {}, {}
        for st in range(len(nrows)):
            for u in range(SB_QTILES):
                lb[(u, st)], hi[(u, st)] = fast_logs(y[(u, st)], causal if st == 0 else None)
        cum3 = cum3_ref[...]

        def later(st):
            n = nrows[st]
            ops = [jnp.concatenate([hi[(u, st - i)][:n] for i in range(st + 1)], axis=1)
                   for u in range(SB_QTILES)]
            out = jnp.dot(jnp.concatenate(ops, axis=0), cum3[:2 * t * (st + 1)],
                          preferred_element_type=F32)
            return [out[u * n:(u + 1) * n] for u in range(SB_QTILES)]

        sums = [later(st) for st in range(len(nrows))]
        w = {}
        for st in range(len(nrows)):
            for u in range(SB_QTILES):
                e = jnp.exp2(lb[(u, st)] + sums[st][u])
                w[(u, st)] = (jnp.where(causal, e, 0.0) if st == 0 else e).astype(BF16)
        acc, acc_top = [None] * SB_QTILES, [None] * SB_QTILES
        for d in offsets:
            ctx = jnp.dot(jnp.concatenate([w[key] for key in members[d]], axis=0), values[d],
                          preferred_element_type=F32)
            off = 0
            for u, st in members[d]:
                part = ctx[off:off + nrows[st]]
                off += nrows[st]
                if st == len(nrows) - 1:
                    acc_top[u] = part
                else:
                    acc[u] = part if acc[u] is None else acc[u] + part
        first_cols = lambda a: jnp.maximum(a[:, 0:1], a[:, t:t + 1])
        bound = None
        for u in range(SB_QTILES):
            b = jnp.maximum(jnp.max(first_cols(sums[2][u])), jnp.max(first_cols(sums[1][u][top:])))
            bound = b if bound is None else jnp.maximum(bound, b)
        return bound, acc, acc_top

    def walk(s, nsteps, run, acc):
        from_diag = isinstance(s, int) and s == 0
        steps = range(nsteps)
        offsets = range(1 - nsteps, SB_QTILES)
        members = {d: [(u, st) for st in steps for u in range(SB_QTILES) if u - st == d]
                   for d in offsets}
        order = [(u, st) for st in steps for u in range(SB_QTILES)]
        is_diag = lambda st: from_diag and st == 0
        y, values = {}, {}
        for d in offsets:
            j = base + d - s
            kbd, vbd = kv(jnp.maximum(j, 0))
            if not (from_diag and d >= 0):
                vbd = jnp.where(j >= 0, vbd, jnp.zeros_like(vbd))
            values[d] = vbd
            stacked = scores(jnp.concatenate([q_tile(u) for u, _ in members[d]], axis=0), kbd)
            for n, key in enumerate(members[d]):
                y[key] = stacked[n * t:(n + 1) * t]
        parts = {key: logs(y[key], causal if is_diag(key[1]) else None) for key in order}
        sums = []
        for st in steps:
            sums.extend(suffix_sums([parts[key][1] for key in order if key[1] == st]))
        run, acc, w = list(run), list(acc), {}
        for key, (later, rowsum) in zip(order, sums):
            u = key[0]
            if is_diag(key[1]):
                w[key] = jnp.where(causal, jnp.exp2(parts[key][0] + later), 0.0).astype(BF16)
                run[u] = rowsum
            else:
                w[key] = jnp.exp2(parts[key][0] + later + run[u]).astype(BF16)
                run[u] = run[u] + rowsum
        for d in offsets:
            ctx = jnp.dot(jnp.concatenate([w[key] for key in members[d]], axis=0), values[d],
                          preferred_element_type=F32)
            for n, (u, _) in enumerate(members[d]):
                part = ctx[n * t:(n + 1) * t]
                acc[u] = part if acc[u] is None else acc[u] + part
        top = run[0]
        for u in range(1, SB_QTILES):
            top = jnp.maximum(top, run[u])
        return jnp.max(top), run, acc

    def general():
        top, run, acc = walk(0, SB_FIRST_STEPS, [None] * SB_QTILES, [None] * SB_QTILES)

        def live(carry):
            return jnp.logical_and(carry[1] > SB_UNDERFLOW, carry[0] <= base + SB_QTILES - 1)

        def body(carry):
            s = carry[0]
            top, run, acc = walk(s, SB_STEPS_PER_CHECK, carry[2:2 + SB_QTILES],
                                 carry[2 + SB_QTILES:])
            return (s + SB_STEPS_PER_CHECK, top, *run, *acc)

        out = lax.while_loop(live, body, (jnp.int32(SB_FIRST_STEPS), top, *run, *acc))
        for u in range(SB_QTILES):
            o_ref[0, u * t:(u + 1) * t, :] = out[2 + SB_QTILES + u].astype(BF16)

    bound, acc, acc_top = fast()
    finished = bound < SB_UNDERFLOW

    @pl.when(finished)
    def _():
        for u in range(SB_QTILES):
            lo_rows = u * t + SB_TOP_ROWS
            o_ref[0, u * t:lo_rows, :] = (acc[u][:SB_TOP_ROWS] + acc_top[u]).astype(BF16)
            o_ref[0, lo_rows:(u + 1) * t, :] = acc[u][SB_TOP_ROWS:].astype(BF16)

    @pl.when(jnp.logical_not(finished))
    def _():
        general()


def _sb_attention(q, k, v, cum, cum3):
    B, S, _ = q.shape
    rows = SB_TILE * SB_QTILES
    pairs = SB_WIDTH // LANES
    qspec = pl.BlockSpec((1, rows, LANES), lambda b, p, i: (b, i, p))
    kvspec = pl.BlockSpec((1, S, 2 * LANES), lambda b, p, i: (b, 0, p))
    return pl.pallas_call(
        _sb_kernel,
        grid=(B, pairs, S // rows),
        in_specs=[qspec, kvspec, kvspec, _resident(cum.shape), _resident(cum3.shape)],
        out_specs=qspec,
        out_shape=jax.ShapeDtypeStruct((B, S, SB_WIDTH), BF16),
        compiler_params=_params("parallel", "parallel", "arbitrary"),
        name="sb_attn",
    )(q, k, v, cum, cum3)


def _ret_kernel(q_ref, k_ref, v_ref, g_ref, decay_ref, qd_ref, kd_ref, cd_ref, o_ref, state_ref):
    @pl.when(pl.program_id(2) == 0)
    def _():
        state_ref[...] = jnp.zeros_like(state_ref)

    c = RET_CHUNK
    for ci in range(RET_ROWS // c):
        sl = slice(ci * c, (ci + 1) * c)
        q = q_ref[0, sl, :]
        k = k_ref[0, sl, :]
        v = v_ref[0, sl, :]
        s = lax.dot_general(q, k, (((1,), (1,)), ((), ())), preferred_element_type=F32) * decay_ref[0]
        inner = jnp.dot(s.astype(BF16), v, preferred_element_type=F32)
        st = state_ref[...]
        cross = jnp.dot(q, st.astype(BF16), preferred_element_type=F32) * qd_ref[0]
        o = inner + cross
        vdec = (v.astype(F32) * kd_ref[0]).astype(BF16)
        state_ref[...] = st * cd_ref[0] + lax.dot_general(
            k, vdec, (((0,), (0,)), ((), ())), preferred_element_type=F32)
        mu = jnp.mean(o, axis=-1, keepdims=True)
        oc = o - mu
        var = jnp.mean(oc * oc, axis=-1, keepdims=True)
        normed = oc * lax.rsqrt(var + LN_EPS)
        o_ref[0, sl, :] = (g_ref[0, sl, :].astype(F32) * normed).astype(BF16)


def _retention(rq, rk, rv, rg, tables):
    B, S, _ = rq.shape
    c = RET_CHUNK
    qk = pl.BlockSpec((1, RET_ROWS, RET_QK_DIM), lambda b, h, r: (b, r, h))
    vv = pl.BlockSpec((1, RET_ROWS, RET_V_DIM), lambda b, h, r: (b, r, h))
    head = lambda rows, cols: pl.BlockSpec((1, rows, cols), lambda b, h, r: (h, 0, 0))
    return pl.pallas_call(
        _ret_kernel,
        grid=(B, RET_HEADS, S // RET_ROWS),
        in_specs=[qk, qk, vv, vv, head(c, c), head(c, RET_V_DIM), head(c, RET_V_DIM),
                  head(RET_QK_DIM, RET_V_DIM)],
        out_specs=vv,
        out_shape=jax.ShapeDtypeStruct((B, S, RET_V_WIDTH), BF16),
        scratch_shapes=[pltpu.VMEM((RET_QK_DIM, RET_V_DIM), F32)],
        compiler_params=_params("parallel", "parallel", "arbitrary"),
        name="retention",
    )(rq, rk, rv, rg, *tables)


def _retention_tables():
    c = RET_CHUNK
    log_gamma = jnp.log1p(-jnp.exp2(-5.0 - jnp.arange(RET_HEADS, dtype=F32)))
    idx = jnp.arange(c, dtype=F32)
    rel = idx[:, None] - idx[None, :]
    lg = log_gamma[:, None, None]
    decay = jnp.where(rel[None] >= 0, jnp.exp(lg * jnp.maximum(rel, 0.0)[None]), 0.0)
    qd = jnp.broadcast_to(jnp.exp(lg * (idx + 1.0)[None, :, None]), (RET_HEADS, c, RET_V_DIM))
    kd = jnp.broadcast_to(jnp.exp(lg * (c - 1.0 - idx)[None, :, None]), (RET_HEADS, c, RET_V_DIM))
    cd = jnp.broadcast_to(jnp.exp(lg * float(c)), (RET_HEADS, RET_QK_DIM, RET_V_DIM))
    return decay, qd, kd, cd


def _mix_kernel(a_ref, r_ref, gate_ref, x_ref, wsb_ref, wret_ref, wmix_ref, g_ref, b_ref, o_ref):
    pending = None
    for sub in range(DENSE_SUBTILES):
        rows = slice(sub * SUB_ROWS, (sub + 1) * SUB_ROWS)
        y_sb = jnp.dot(a_ref[rows, :], wsb_ref[...], preferred_element_type=F32)
        y_ret = jnp.dot(r_ref[rows, :], wret_ref[...], preferred_element_type=F32)
        merged = (gate_ref[rows, :D_MODEL].astype(F32) * y_sb
                  + gate_ref[rows, D_MODEL:].astype(F32) * y_ret)
        mix = jnp.dot(merged.astype(BF16), wmix_ref[...], preferred_element_type=F32)
        if pending is not None:
            o_ref[pending[0], :] = _layer_norm(pending[1], g_ref[...], b_ref[...])
        pending = (rows, DN_ALPHA * x_ref[rows, :] + mix)
    o_ref[pending[0], :] = _layer_norm(pending[1], g_ref[...], b_ref[...])


def _mix_ln1(attn, retg, gates, x2d, w_sb_o, w_ret_o, w_mix_o, g, b):
    T = x2d.shape[0]
    tm = SUB_ROWS * DENSE_SUBTILES
    row = lambda w: pl.BlockSpec((tm, w), lambda i: (i, 0))
    return pl.pallas_call(
        _mix_kernel,
        grid=(T // tm,),
        in_specs=[row(SB_WIDTH), row(RET_V_WIDTH), row(N_BRANCHES * D_MODEL), row(D_MODEL),
                  _resident(w_sb_o.shape), _resident(w_ret_o.shape), _resident(w_mix_o.shape),
                  _resident(g.shape), _resident(b.shape)],
        out_specs=row(D_MODEL),
        out_shape=jax.ShapeDtypeStruct((T, D_MODEL), F32),
        compiler_params=_params("parallel"),
        name="mix_ln1",
    )(attn, retg, gates, x2d, w_sb_o, w_ret_o, w_mix_o, g, b)


def _memkv_kernel(m_ref, w_ref, k_ref, v_ref):
    mb = m_ref[0].astype(BF16)
    k_ref[0] = jnp.dot(mb, w_ref[:, :D_MODEL], preferred_element_type=F32).astype(BF16)
    v_ref[0] = jnp.dot(mb, w_ref[:, D_MODEL:], preferred_element_type=F32).astype(BF16)


def _mem_kv(mem, w_kv):
    B, M, _ = mem.shape
    blk = pl.BlockSpec((1, M, D_MODEL), lambda b: (b, 0, 0))
    return pl.pallas_call(
        _memkv_kernel,
        grid=(B,),
        in_specs=[blk, _resident(w_kv.shape)],
        out_specs=[blk, blk],
        out_shape=[jax.ShapeDtypeStruct((B, M, D_MODEL), BF16)] * 2,
        compiler_params=_params("parallel"),
        name="mem_kv",
    )(mem, w_kv)


def _xattn_kernel(x_ref, k_ref, v_ref, wq_ref, wo_ref, g_ref, b_ref, o_ref, ctx_ref):
    for sub in range(DENSE_SUBTILES):
        rows = slice(sub * SUB_ROWS, (sub + 1) * SUB_ROWS)
        x = x_ref[0, rows, :]
        q = (jnp.dot(x.astype(BF16), wq_ref[...], preferred_element_type=F32)
             * (MEM_HEAD_DIM ** -0.5)).astype(BF16)
        for h in range(MEM_HEADS):
            sl = slice(h * MEM_HEAD_DIM, (h + 1) * MEM_HEAD_DIM)
            s = lax.dot_general(q[:, sl], k_ref[0, :, sl], (((1,), (1,)), ((), ())),
                                preferred_element_type=F32)
            p = jnp.exp(s - jnp.max(s, axis=-1, keepdims=True))
            denom = jnp.sum(p, axis=-1, keepdims=True)
            ctx = jnp.dot(p.astype(BF16), v_ref[0, :, sl], preferred_element_type=F32)
            ctx_ref[rows, sl] = (ctx / denom).astype(BF16)
        xa = jnp.dot(ctx_ref[rows, :], wo_ref[...], preferred_element_type=F32)
        o_ref[0, rows, :] = _layer_norm(DN_ALPHA * x + xa, g_ref[...], b_ref[...])


def _xattn_ln2(x3d, mk, mv, w_q, w_o, g, b):
    B, S, _ = x3d.shape
    M = mk.shape[1]
    tm = SUB_ROWS * DENSE_SUBTILES
    row = pl.BlockSpec((1, tm, D_MODEL), lambda bi, i: (bi, i, 0))
    kv = pl.BlockSpec((1, M, D_MODEL), lambda bi, i: (bi, 0, 0))
    return pl.pallas_call(
        _xattn_kernel,
        grid=(B, S // tm),
        in_specs=[row, kv, kv, _resident(w_q.shape), _resident(w_o.shape),
                  _resident(g.shape), _resident(b.shape)],
        out_specs=row,
        out_shape=jax.ShapeDtypeStruct((B, S, D_MODEL), F32),
        scratch_shapes=[pltpu.VMEM((tm, D_MODEL), BF16)],
        compiler_params=_params("parallel", "arbitrary"),
        name="xattn_ln2",
    )(x3d, mk, mv, w_q, w_o, g, b)


def _ffn_kernel(x_ref, win_ref, wout_ref, g_ref, b_ref, o_ref, hid_ref):
    x = x_ref[...]
    xb = x.astype(BF16)
    for c in range(FFN_HIDDEN // FFN_CHUNK):
        lo = c * FFN_CHUNK
        a = jnp.dot(xb, win_ref[:, lo:lo + FFN_CHUNK], preferred_element_type=F32)
        gte = jnp.dot(xb, win_ref[:, FFN_HIDDEN + lo:FFN_HIDDEN + lo + FFN_CHUNK],
                      preferred_element_type=F32)
        hid_ref[:, lo:lo + FFN_CHUNK] = (a * jax.nn.sigmoid(a) * gte).astype(BF16)
    ff = jnp.dot(hid_ref[...], wout_ref[...], preferred_element_type=F32)
    o_ref[...] = _layer_norm(DN_ALPHA * x + ff, g_ref[...], b_ref[...])


def _ffn_ln3(x2d, w_in, w_out, g, b):
    T = x2d.shape[0]
    tm = ROW_TILE
    row = pl.BlockSpec((tm, D_MODEL), lambda i: (i, 0))
    return pl.pallas_call(
        _ffn_kernel,
        grid=(T // tm,),
        in_specs=[row, _resident(w_in.shape), _resident(w_out.shape),
                  _resident(g.shape), _resident(b.shape)],
        out_specs=row,
        out_shape=jax.ShapeDtypeStruct((T, D_MODEL), F32),
        scratch_shapes=[pltpu.VMEM((tm, FFN_HIDDEN), BF16)],
        compiler_params=_params("parallel"),
        name="ffn_ln3",
    )(x2d, w_in, w_out, g, b)


def _rope_tables(seq):
    half = RET_QK_DIM // 2
    inv = 1.0 / (ROPE_BASE ** (jnp.arange(half, dtype=F32) / half))
    ang = jnp.arange(seq, dtype=F32)[:, None] * inv[None, :]
    cos = jnp.cos(ang)
    sin = jnp.sin(ang)
    return jnp.concatenate([cos, cos], axis=-1), jnp.concatenate([-sin, sin], axis=-1)


def _cumsum_matrices():
    t = SB_TILE
    r = jnp.arange(t)
    upper = (r[:, None] > r[None, :]).astype(BF16)
    ones = jnp.ones((t, t), BF16)
    zero = jnp.zeros((t, t), BF16)
    half = jnp.concatenate([upper, ones], axis=1)
    cum = jnp.concatenate([half, half], axis=0)
    both = lambda m: jnp.concatenate([jnp.concatenate([m, zero], axis=1),
                                      jnp.concatenate([zero, m], axis=1)], axis=0)
    cum3 = jnp.concatenate([both(upper)] + [both(ones)] * 2, axis=0)
    return cum, cum3


def kernel(x, mem, w_in, b_gate, w_sb_o, w_ret_o, w_mix_o, ln1_g, ln1_b, w_mem_q, w_mem_kv,
           w_mem_o, ln2_g, ln2_b, w_ffn_in, w_ffn_out, ln3_g, ln3_b):
    B, S, D = x.shape
    assert D == D_MODEL and w_in.shape == (DEPTH, D_MODEL, IN_WIDTH)
    assert S % RET_ROWS == 0 and (B * S) % ROW_TILE == 0 and S % ROW_TILE == 0
    cos_t, sin_t = _rope_tables(S)
    cum, cum3 = _cumsum_matrices()
    ret_tables = _retention_tables()
    x2d = x.reshape(B * S, D)
    for l in range(DEPTH):
        bf = lambda w: w[l].astype(BF16)
        vec = lambda p: p[l][None, :]
        sbq, sbk, sbv, rq, rk, rv, rg, gates = _inproj(
            x2d, bf(w_in), vec(b_gate), cos_t, sin_t, S)
        seq3 = lambda a: a.reshape(B, S, a.shape[-1])
        attn = _sb_attention(seq3(sbq), seq3(sbk), seq3(sbv), cum, cum3)
        retg = _retention(seq3(rq), seq3(rk), seq3(rv), seq3(rg), ret_tables)
        x1 = _mix_ln1(attn.reshape(B * S, SB_WIDTH), retg.reshape(B * S, RET_V_WIDTH), gates, x2d,
                      bf(w_sb_o), bf(w_ret_o), bf(w_mix_o), vec(ln1_g), vec(ln1_b))
        mk, mv = _mem_kv(mem, bf(w_mem_kv))
        x2 = _xattn_ln2(x1.reshape(B, S, D), mk, mv, bf(w_mem_q), bf(w_mem_o),
                        vec(ln2_g), vec(ln2_b))
        x2d = _ffn_ln3(x2.reshape(B * S, D), bf(w_ffn_in), bf(w_ffn_out), vec(ln3_g), vec(ln3_b))
    return x2d.reshape(B, S, D)
```

```python
import functools

import jax
import jax.numpy as jnp
from jax import lax
from jax.experimental import pallas as pl
from jax.experimental.pallas import tpu as pltpu

F32 = jnp.float32
BF16 = jnp.bfloat16

D_MODEL = 1024
DEPTH = 1
SB_HEADS = 8
SB_HEAD_DIM = 64
SB_WIDTH = SB_HEADS * SB_HEAD_DIM
RET_HEADS = 4
RET_QK_DIM = 128
RET_V_DIM = 256
RET_QK_WIDTH = RET_HEADS * RET_QK_DIM
RET_V_WIDTH = RET_HEADS * RET_V_DIM
ROPE_BASE = 10000.0
N_BRANCHES = 2
OFF_SB_Q = 0
OFF_SB_K = OFF_SB_Q + SB_WIDTH
OFF_SB_V = OFF_SB_K + SB_WIDTH
OFF_RET_Q = OFF_SB_V + SB_WIDTH
OFF_RET_K = OFF_RET_Q + RET_QK_WIDTH
OFF_RET_V = OFF_RET_K + RET_QK_WIDTH
OFF_RET_G = OFF_RET_V + RET_V_WIDTH
OFF_GATE = OFF_RET_G + RET_V_WIDTH
IN_WIDTH = OFF_GATE + N_BRANCHES * D_MODEL
MEM_HEADS = 4
MEM_HEAD_DIM = D_MODEL // MEM_HEADS
FFN_HIDDEN = 2816
DN_ALPHA = (2.0 * DEPTH) ** 0.25
LN_EPS = 1e-5

LANES = 128
VMEM_LIMIT_BYTES = 56 * 1024 * 1024
ROW_TILE = 512
SUB_ROWS = 256
DENSE_SUBTILES = 4
SB_TILE = 128
SB_QTILES = 8
SB_TOP_ROWS = 32
SB_FIRST_STEPS = 3
SB_STEPS_PER_CHECK = 2
SB_UNDERFLOW = -150.5
LOG2_E = 1.4426950408889634
RET_CHUNK = 256
RET_ROWS = 4096
FFN_CHUNK = 256


def _resident(shape):
    zeros = (0,) * len(shape)
    return pl.BlockSpec(shape, lambda *_: zeros, pipeline_mode=pl.Buffered(1))


def _params(*sem):
    return pltpu.CompilerParams(dimension_semantics=sem, vmem_limit_bytes=VMEM_LIMIT_BYTES)


def _layer_norm(y, g, b):
    mu = jnp.mean(y, axis=-1, keepdims=True)
    yc = y - mu
    var = jnp.mean(yc * yc, axis=-1, keepdims=True)
    return yc * lax.rsqrt(var + LN_EPS) * g + b


def _inproj_kernel(x_ref, w_ref, bg_ref, cos_ref, sin_ref,
                   sbq_ref, sbk_ref, sbv_ref, rq_ref, rk_ref, rv_ref, rg_ref, gate_ref):
    xb = x_ref[...].astype(BF16)

    def proj(off, width):
        return jnp.dot(xb, w_ref[:, off:off + width], preferred_element_type=F32)

    sbq_ref[...] = (proj(OFF_SB_Q, SB_WIDTH) * (SB_HEAD_DIM ** -0.5 * LOG2_E)).astype(BF16)

    def store_head_split(h, out_ref):
        hb = h.astype(BF16)
        first = lax.broadcasted_iota(jnp.int32, (hb.shape[0], LANES), 1) < SB_HEAD_DIM
        zero = jnp.zeros((hb.shape[0], LANES), BF16)
        for p in range(SB_WIDTH // LANES):
            pair = hb[:, p * LANES:(p + 1) * LANES]
            out_ref[:, 2 * p * LANES:(2 * p + 1) * LANES] = jnp.where(first, pair, zero)
            out_ref[:, (2 * p + 1) * LANES:(2 * p + 2) * LANES] = jnp.where(first, zero, pair)

    store_head_split(proj(OFF_SB_K, SB_WIDTH), sbk_ref)
    store_head_split(proj(OFF_SB_V, SB_WIDTH), sbv_ref)

    cos = cos_ref[...]
    sin = sin_ref[...]

    def rope_store(h, out_ref, scale):
        for hd in range(RET_HEADS):
            sl = slice(hd * RET_QK_DIM, (hd + 1) * RET_QK_DIM)
            xh = h[:, sl]
            rot = xh * cos + pltpu.roll(xh, RET_QK_DIM // 2, 1) * sin
            if scale != 1.0:
                rot = rot * scale
            out_ref[:, sl] = rot.astype(BF16)

    rope_store(proj(OFF_RET_Q, RET_QK_WIDTH), rq_ref, RET_QK_DIM ** -0.5)
    rope_store(proj(OFF_RET_K, RET_QK_WIDTH), rk_ref, 1.0)
    rv_ref[...] = proj(OFF_RET_V, RET_V_WIDTH).astype(BF16)
    g = proj(OFF_RET_G, RET_V_WIDTH)
    rg_ref[...] = (g * jax.nn.sigmoid(g)).astype(BF16)
    for br in range(N_BRANCHES):
        sl = slice(br * D_MODEL, (br + 1) * D_MODEL)
        hg = proj(OFF_GATE + br * D_MODEL, D_MODEL) + bg_ref[:, sl]
        gate_ref[:, sl] = jax.nn.sigmoid(hg).astype(BF16)


def _inproj(x2d, w_in, b_gate, cos_t, sin_t, seq):
    T = x2d.shape[0]
    tm = ROW_TILE
    tiles_per_seq = seq // tm
    row = lambda w: pl.BlockSpec((tm, w), lambda i: (i, 0))
    tab = pl.BlockSpec((tm, LANES), lambda i: (i % tiles_per_seq, 0))
    widths = (SB_WIDTH, 2 * SB_WIDTH, 2 * SB_WIDTH, RET_QK_WIDTH, RET_QK_WIDTH,
              RET_V_WIDTH, RET_V_WIDTH, N_BRANCHES * D_MODEL)
    return pl.pallas_call(
        _inproj_kernel,
        grid=(T // tm,),
        in_specs=[row(D_MODEL), _resident((D_MODEL, IN_WIDTH)),
                  _resident((1, N_BRANCHES * D_MODEL)), tab, tab],
        out_specs=[row(w) for w in widths],
        out_shape=[jax.ShapeDtypeStruct((T, w), BF16) for w in widths],
        compiler_params=_params("parallel"),
        name="inproj",
    )(x2d, w_in, b_gate, cos_t, sin_t)


def _sb_kernel(q_ref, k_ref, v_ref, cum_ref, cum3_ref, o_ref):
    blk = pl.program_id(2)
    t = SB_TILE
    cum = cum_ref[...]
    row = lax.broadcasted_iota(jnp.int32, (t, 2 * t), 0)
    col = lax.broadcasted_iota(jnp.int32, (t, 2 * t), 1)
    causal = jnp.where(col >= t, col - t, col) < row

    def stack(x):
        return jnp.concatenate([x[:, :LANES], x[:, LANES:]], axis=0)

    def kv(j):
        start = pl.multiple_of(j * t, t)
        return stack(k_ref[0, pl.ds(start, t), :]), stack(v_ref[0, pl.ds(start, t), :])

    def q_tile(u):
        return q_ref[0, u * t:(u + 1) * t, :]

    def scores(q_rows, kbd):
        return lax.dot_general(q_rows, kbd, (((1,), (1,)), ((), ())), preferred_element_type=F32)

    def logs(y, mask):
        log_beta = jnp.minimum(y, 0.0) - jnp.log2(1.0 + jnp.exp2(-jnp.abs(y)))
        log_rem = log_beta - y
        if mask is not None:
            log_rem = jnp.where(mask, log_rem, 0.0)
        hi = log_rem.astype(BF16)
        lo = (log_rem - hi.astype(F32)).astype(BF16)
        rows = [jnp.concatenate([hi[:, h * t:(h + 1) * t], lo[:, h * t:(h + 1) * t]], axis=1)
                for h in range(2)]
        return log_beta, jnp.concatenate(rows, axis=0)

    def suffix_sums(operands):
        sums = jnp.dot(jnp.concatenate(operands, axis=0), cum, preferred_element_type=F32)
        out = []
        for n in range(len(operands)):
            a = sums[2 * n * t:(2 * n + 1) * t]
            b = sums[(2 * n + 1) * t:(2 * n + 2) * t]
            out.append((jnp.concatenate([a[:, :t], b[:, :t]], axis=1),
                        jnp.concatenate([a[:, t:], b[:, t:]], axis=1)))
        return out

    base = blk * SB_QTILES

    def fast_logs(y, mask):
        log_beta = jnp.minimum(y, 0.0) - jnp.log2(1.0 + jnp.exp2(-jnp.abs(y)))
        log_rem = log_beta - y
        if mask is not None:
            log_rem = jnp.where(mask, log_rem, 0.0)
        return log_beta, log_rem.astype(BF16)

    def fast():
        top = SB_TOP_ROWS
        nrows = (t, t, top)
        offsets = range(1 - len(nrows), SB_QTILES)
        members = {d: [(d + st, st) for st in range(len(nrows)) if 0 <= d + st < SB_QTILES]
                   for d in offsets}
        y, values = {}, {}
        for d in offsets:
            j = base + d
            kbd, vbd = kv(jnp.maximum(j, 0))
            if d < 0:
                vbd = jnp.where(j >= 0, vbd, jnp.zeros_like(vbd))
            values[d] = vbd
            q_rows = [q_ref[0, u * t:u * t + nrows[st], :] for u, st in members[d]]
            stacked = scores(jnp.concatenate(q_rows, axis=0), kbd)
            off = 0
            for u, st in members[d]:
                y[(u, st)] = stacked[off:off + nrows[st]]
                off += nrows[st]
        lb, hi = {}, {}
        for st in range(len(nrows)):
            for u in range(SB_QTILES):
                lb[(u, st)], hi[(u, st)] = fast_logs(y[(u, st)], causal if st == 0 else None)
        cum3 = cum3_ref[...]

        def later(st):
            n = nrows[st]
            ops = [jnp.concatenate([hi[(u, st - i)][:n] for i in range(st + 1)], axis=1)
                   for u in range(SB_QTILES)]
            out = jnp.dot(jnp.concatenate(ops, axis=0), cum3[:2 * t * (st + 1)],
                          preferred_element_type=F32)
            return [out[u * n:(u + 1) * n] for u in range(SB_QTILES)]

        sums = [later(st) for st in range(len(nrows))]
        w = {}
        for st in range(len(nrows)):
            for u in range(SB_QTILES):
                e = jnp.exp2(lb[(u, st)] + sums[st][u])
                w[(u, st)] = (jnp.where(causal, e, 0.0) if st == 0 else e).astype(BF16)
        acc, acc_top = [None] * SB_QTILES, [None] * SB_QTILES
        for d in offsets:
            ctx = jnp.dot(jnp.concatenate([w[key] for key in members[d]], axis=0), values[d],
                          preferred_element_type=F32)
            off = 0
            for u, st in members[d]:
                part = ctx[off:off + nrows[st]]
                off += nrows[st]
                if st == len(nrows) - 1:
                    acc_top[u] = part
                else:
                    acc[u] = part if acc[u] is None else acc[u] + part
        first_cols = lambda a: jnp.maximum(a[:, 0:1], a[:, t:t + 1])
        bound = None
        for u in range(SB_QTILES):
            b = jnp.maximum(jnp.max(first_cols(sums[2][u])), jnp.max(first_cols(sums[1][u][top:])))
            bound = b if bound is None else jnp.maximum(bound, b)
        return bound, acc, acc_top

    def walk(s, nsteps, run, acc):
        from_diag = isinstance(s, int) and s == 0
        steps = range(nsteps)
        offsets = range(1 - nsteps, SB_QTILES)
        members = {d: [(u, st) for st in steps for u in range(SB_QTILES) if u - st == d]
                   for d in offsets}
        order = [(u, st) for st in steps for u in range(SB_QTILES)]
        is_diag = lambda st: from_diag and st == 0
        y, values = {}, {}
        for d in offsets:
            j = base + d - s
            kbd, vbd = kv(jnp.maximum(j, 0))
            if not (from_diag and d >= 0):
                vbd = jnp.where(j >= 0, vbd, jnp.zeros_like(vbd))
            values[d] = vbd
            stacked = scores(jnp.concatenate([q_tile(u) for u, _ in members[d]], axis=0), kbd)
            for n, key in enumerate(members[d]):
                y[key] = stacked[n * t:(n + 1) * t]
        parts = {key: logs(y[key], causal if is_diag(key[1]) else None) for key in order}
        sums = []
        for st in steps:
            sums.extend(suffix_sums([parts[key][1] for key in order if key[1] == st]))
        run, acc, w = list(run), list(acc), {}
        for key, (later, rowsum) in zip(order, sums):
            u = key[0]
            if is_diag(key[1]):
                w[key] = jnp.where(causal, jnp.exp2(parts[key][0] + later), 0.0).astype(BF16)
                run[u] = rowsum
            else:
                w[key] = jnp.exp2(parts[key][0] + later + run[u]).astype(BF16)
                run[u] = run[u] + rowsum
        for d in offsets:
            ctx = jnp.dot(jnp.concatenate([w[key] for key in members[d]], axis=0), values[d],
                          preferred_element_type=F32)
            for n, (u, _) in enumerate(members[d]):
                part = ctx[n * t:(n + 1) * t]
                acc[u] = part if acc[u] is None else acc[u] + part
        top = run[0]
        for u in range(1, SB_QTILES):
            top = jnp.maximum(top, run[u])
        return jnp.max(top), run, acc

    def general():
        top, run, acc = walk(0, SB_FIRST_STEPS, [None] * SB_QTILES, [None] * SB_QTILES)

        def live(carry):
            return jnp.logical_and(carry[1] > SB_UNDERFLOW, carry[0] <= base + SB_QTILES - 1)

        def body(carry):
            s = carry[0]
            top, run, acc = walk(s, SB_STEPS_PER_CHECK, carry[2:2 + SB_QTILES],
                                 carry[2 + SB_QTILES:])
            return (s + SB_STEPS_PER_CHECK, top, *run, *acc)

        out = lax.while_loop(live, body, (jnp.int32(SB_FIRST_STEPS), top, *run, *acc))
        for u in range(SB_QTILES):
            o_ref[0, u * t:(u + 1) * t, :] = out[2 + SB_QTILES + u].astype(BF16)

    bound, acc, acc_top = fast()
    finished = bound < SB_UNDERFLOW

    @pl.when(finished)
    def _():
        for u in range(SB_QTILES):
            lo_rows = u * t + SB_TOP_ROWS
            o_ref[0, u * t:lo_rows, :] = (acc[u][:SB_TOP_ROWS] + acc_top[u]).astype(BF16)
            o_ref[0, lo_rows:(u + 1) * t, :] = acc[u][SB_TOP_ROWS:].astype(BF16)

    @pl.when(jnp.logical_not(finished))
    def _():
        general()


def _sb_attention(q, k, v, cum, cum3):
    B, S, _ = q.shape
    rows = SB_TILE * SB_QTILES
    pairs = SB_WIDTH // LANES
    qspec = pl.BlockSpec((1, rows, LANES), lambda b, p, i: (b, i, p))
    kvspec = pl.BlockSpec((1, S, 2 * LANES), lambda b, p, i: (b, 0, p))
    return pl.pallas_call(
        _sb_kernel,
        grid=(B, pairs, S // rows),
        in_specs=[qspec, kvspec, kvspec, _resident(cum.shape), _resident(cum3.shape)],
        out_specs=qspec,
        out_shape=jax.ShapeDtypeStruct((B, S, SB_WIDTH), BF16),
        compiler_params=_params("parallel", "parallel", "arbitrary"),
        name="sb_attn",
    )(q, k, v, cum, cum3)


def _ret_kernel(q_ref, k_ref, v_ref, g_ref, decay_ref, qd_ref, kd_ref, cd_ref, o_ref, state_ref):
    @pl.when(pl.program_id(2) == 0)
    def _():
        state_ref[...] = jnp.zeros_like(state_ref)

    c = RET_CHUNK
    chunks = [slice(ci * c, (ci + 1) * c) for ci in range(RET_ROWS // c)]
    inner, chunk_kv = [], []
    for sl in chunks:
        q = q_ref[0, sl, :]
        k = k_ref[0, sl, :]
        v = v_ref[0, sl, :]
        s = lax.dot_general(q, k, (((1,), (1,)), ((), ())), preferred_element_type=F32) * decay_ref[0]
        inner.append(jnp.dot(s.astype(BF16), v, preferred_element_type=F32))
        kdec = (k.astype(F32) * kd_ref[0]).astype(BF16)
        chunk_kv.append(lax.dot_general(kdec, v, (((0,), (0,)), ((), ())),
                                        preferred_element_type=F32))
    st = state_ref[...]
    before = []
    for kv in chunk_kv:
        before.append(st.astype(BF16))
        st = st * cd_ref[0] + kv
    state_ref[...] = st
    for sl, part, state in zip(chunks, inner, before):
        o = part + jnp.dot(q_ref[0, sl, :], state, preferred_element_type=F32) * qd_ref[0]
        mu = jnp.mean(o, axis=-1, keepdims=True)
        oc = o - mu
        var = jnp.mean(oc * oc, axis=-1, keepdims=True)
        normed = oc * lax.rsqrt(var + LN_EPS)
        o_ref[0, sl, :] = normed.astype(BF16) * g_ref[0, sl, :]


def _retention(rq, rk, rv, rg, tables):
    B, S, _ = rq.shape
    c = RET_CHUNK
    qk = pl.BlockSpec((1, RET_ROWS, RET_QK_DIM), lambda b, h, r: (b, r, h))
    vv = pl.BlockSpec((1, RET_ROWS, RET_V_DIM), lambda b, h, r: (b, r, h))
    head = lambda rows, cols: pl.BlockSpec((1, rows, cols), lambda b, h, r: (h, 0, 0))
    return pl.pallas_call(
        _ret_kernel,
        grid=(B, RET_HEADS, S // RET_ROWS),
        in_specs=[qk, qk, vv, vv, head(c, c), head(c, RET_V_DIM), head(c, RET_QK_DIM),
                  head(RET_QK_DIM, RET_V_DIM)],
        out_specs=vv,
        out_shape=jax.ShapeDtypeStruct((B, S, RET_V_WIDTH), BF16),
        scratch_shapes=[pltpu.VMEM((RET_QK_DIM, RET_V_DIM), F32)],
        compiler_params=_params("parallel", "parallel", "arbitrary"),
        name="retention",
    )(rq, rk, rv, rg, *tables)


def _retention_tables():
    c = RET_CHUNK
    log_gamma = jnp.log1p(-jnp.exp2(-5.0 - jnp.arange(RET_HEADS, dtype=F32)))
    idx = jnp.arange(c, dtype=F32)
    rel = idx[:, None] - idx[None, :]
    lg = log_gamma[:, None, None]
    decay = jnp.where(rel[None] >= 0, jnp.exp(lg * jnp.maximum(rel, 0.0)[None]), 0.0)
    qd = jnp.broadcast_to(jnp.exp(lg * (idx + 1.0)[None, :, None]), (RET_HEADS, c, RET_V_DIM))
    kd = jnp.broadcast_to(jnp.exp(lg * (c - 1.0 - idx)[None, :, None]), (RET_HEADS, c, RET_QK_DIM))
    cd = jnp.broadcast_to(jnp.exp(lg * float(c)), (RET_HEADS, RET_QK_DIM, RET_V_DIM))
    return decay, qd, kd, cd


def _mix_kernel(a_ref, r_ref, gate_ref, x_ref, wsb_ref, wret_ref, wmix_ref, g_ref, b_ref, o_ref):
    pending = None
    for sub in range(DENSE_SUBTILES):
        rows = slice(sub * SUB_ROWS, (sub + 1) * SUB_ROWS)
        y_sb = jnp.dot(a_ref[rows, :], wsb_ref[...], preferred_element_type=F32)
        y_ret = jnp.dot(r_ref[rows, :], wret_ref[...], preferred_element_type=F32)
        merged = (gate_ref[rows, :D_MODEL].astype(F32) * y_sb
                  + gate_ref[rows, D_MODEL:].astype(F32) * y_ret)
        mix = jnp.dot(merged.astype(BF16), wmix_ref[...], preferred_element_type=F32)
        if pending is not None:
            o_ref[pending[0], :] = _layer_norm(pending[1], g_ref[...], b_ref[...])
        pending = (rows, DN_ALPHA * x_ref[rows, :] + mix)
    o_ref[pending[0], :] = _layer_norm(pending[1], g_ref[...], b_ref[...])


def _mix_ln1(attn, retg, gates, x2d, w_sb_o, w_ret_o, w_mix_o, g, b):
    T = x2d.shape[0]
    tm = SUB_ROWS * DENSE_SUBTILES
    row = lambda w: pl.BlockSpec((tm, w), lambda i: (i, 0))
    return pl.pallas_call(
        _mix_kernel,
        grid=(T // tm,),
        in_specs=[row(SB_WIDTH), row(RET_V_WIDTH), row(N_BRANCHES * D_MODEL), row(D_MODEL),
                  _resident(w_sb_o.shape), _resident(w_ret_o.shape), _resident(w_mix_o.shape),
                  _resident(g.shape), _resident(b.shape)],
        out_specs=row(D_MODEL),
        out_shape=jax.ShapeDtypeStruct((T, D_MODEL), F32),
        compiler_params=_params("parallel"),
        name="mix_ln1",
    )(attn, retg, gates, x2d, w_sb_o, w_ret_o, w_mix_o, g, b)


def _memkv_kernel(m_ref, w_ref, k_ref, v_ref):
    mb = m_ref[0].astype(BF16)
    k_ref[0] = jnp.dot(mb, w_ref[:, :D_MODEL], preferred_element_type=F32).astype(BF16)
    v_ref[0] = jnp.dot(mb, w_ref[:, D_MODEL:], preferred_element_type=F32).astype(BF16)


def _mem_kv(mem, w_kv):
    B, M, _ = mem.shape
    blk = pl.BlockSpec((1, M, D_MODEL), lambda b: (b, 0, 0))
    return pl.pallas_call(
        _memkv_kernel,
        grid=(B,),
        in_specs=[blk, _resident(w_kv.shape)],
        out_specs=[blk, blk],
        out_shape=[jax.ShapeDtypeStruct((B, M, D_MODEL), BF16)] * 2,
        compiler_params=_params("parallel"),
        name="mem_kv",
    )(mem, w_kv)


def _xattn_kernel(x_ref, k_ref, v_ref, wq_ref, wo_ref, g_ref, b_ref, o_ref, ctx_ref):
    for sub in range(DENSE_SUBTILES):
        rows = slice(sub * SUB_ROWS, (sub + 1) * SUB_ROWS)
        x = x_ref[0, rows, :]
        q = (jnp.dot(x.astype(BF16), wq_ref[...], preferred_element_type=F32)
             * (MEM_HEAD_DIM ** -0.5)).astype(BF16)
        for h in range(MEM_HEADS):
            sl = slice(h * MEM_HEAD_DIM, (h + 1) * MEM_HEAD_DIM)
            s = lax.dot_general(q[:, sl], k_ref[0, :, sl], (((1,), (1,)), ((), ())),
                                preferred_element_type=F32)
            p = jnp.exp(s - jnp.max(s, axis=-1, keepdims=True))
            denom = jnp.sum(p, axis=-1, keepdims=True)
            ctx = jnp.dot(p.astype(BF16), v_ref[0, :, sl], preferred_element_type=F32)
            ctx_ref[rows, sl] = (ctx / denom).astype(BF16)
        xa = jnp.dot(ctx_ref[rows, :], wo_ref[...], preferred_element_type=F32)
        o_ref[0, rows, :] = _layer_norm(DN_ALPHA * x + xa, g_ref[...], b_ref[...])


def _xattn_ln2(x3d, mk, mv, w_q, w_o, g, b):
    B, S, _ = x3d.shape
    M = mk.shape[1]
    tm = SUB_ROWS * DENSE_SUBTILES
    row = pl.BlockSpec((1, tm, D_MODEL), lambda bi, i: (bi, i, 0))
    kv = pl.BlockSpec((1, M, D_MODEL), lambda bi, i: (bi, 0, 0))
    return pl.pallas_call(
        _xattn_kernel,
        grid=(B, S // tm),
        in_specs=[row, kv, kv, _resident(w_q.shape), _resident(w_o.shape),
                  _resident(g.shape), _resident(b.shape)],
        out_specs=row,
        out_shape=jax.ShapeDtypeStruct((B, S, D_MODEL), F32),
        scratch_shapes=[pltpu.VMEM((tm, D_MODEL), BF16)],
        compiler_params=_params("parallel", "arbitrary"),
        name="xattn_ln2",
    )(x3d, mk, mv, w_q, w_o, g, b)


def _ffn_kernel(x_ref, win_ref, wout_ref, g_ref, b_ref, o_ref, hid_ref):
    x = x_ref[...]
    xb = x.astype(BF16)
    for c in range(FFN_HIDDEN // FFN_CHUNK):
        lo = c * FFN_CHUNK
        a = jnp.dot(xb, win_ref[:, lo:lo + FFN_CHUNK], preferred_element_type=F32)
        gte = jnp.dot(xb, win_ref[:, FFN_HIDDEN + lo:FFN_HIDDEN + lo + FFN_CHUNK],
                      preferred_element_type=F32)
        hid_ref[:, lo:lo + FFN_CHUNK] = (a * jax.nn.sigmoid(a) * gte).astype(BF16)
    ff = jnp.dot(hid_ref[...], wout_ref[...], preferred_element_type=F32)
    o_ref[...] = _layer_norm(DN_ALPHA * x + ff, g_ref[...], b_ref[...])


def _ffn_ln3(x2d, w_in, w_out, g, b):
    T = x2d.shape[0]
    tm = ROW_TILE
    row = pl.BlockSpec((tm, D_MODEL), lambda i: (i, 0))
    return pl.pallas_call(
        _ffn_kernel,
        grid=(T // tm,),
        in_specs=[row, _resident(w_in.shape), _resident(w_out.shape),
                  _resident(g.shape), _resident(b.shape)],
        out_specs=row,
        out_shape=jax.ShapeDtypeStruct((T, D_MODEL), F32),
        scratch_shapes=[pltpu.VMEM((tm, FFN_HIDDEN), BF16)],
        compiler_params=_params("parallel"),
        name="ffn_ln3",
    )(x2d, w_in, w_out, g, b)


def _rope_tables(seq):
    half = RET_QK_DIM // 2
    inv = 1.0 / (ROPE_BASE ** (jnp.arange(half, dtype=F32) / half))
    ang = jnp.arange(seq, dtype=F32)[:, None] * inv[None, :]
    cos = jnp.cos(ang)
    sin = jnp.sin(ang)
    return jnp.concatenate([cos, cos], axis=-1), jnp.concatenate([-sin, sin], axis=-1)


def _cumsum_matrices():
    t = SB_TILE
    r = jnp.arange(t)
    upper = (r[:, None] > r[None, :]).astype(BF16)
    ones = jnp.ones((t, t), BF16)
    zero = jnp.zeros((t, t), BF16)
    half = jnp.concatenate([upper, ones], axis=1)
    cum = jnp.concatenate([half, half], axis=0)
    both = lambda m: jnp.concatenate([jnp.concatenate([m, zero], axis=1),
                                      jnp.concatenate([zero, m], axis=1)], axis=0)
    cum3 = jnp.concatenate([both(upper)] + [both(ones)] * 2, axis=0)
    return cum, cum3


def kernel(x, mem, w_in, b_gate, w_sb_o, w_ret_o, w_mix_o, ln1_g, ln1_b, w_mem_q, w_mem_kv,
           w_mem_o, ln2_g, ln2_b, w_ffn_in, w_ffn_out, ln3_g, ln3_b):
    B, S, D = x.shape
    assert D == D_MODEL and w_in.shape == (DEPTH, D_MODEL, IN_WIDTH)
    assert S % RET_ROWS == 0 and (B * S) % ROW_TILE == 0 and S % ROW_TILE == 0
    cos_t, sin_t = _rope_tables(S)
    cum, cum3 = _cumsum_matrices()
    ret_tables = _retention_tables()
    x2d = x.reshape(B * S, D)
    for l in range(DEPTH):
        bf = lambda w: w[l].astype(BF16)
        vec = lambda p: p[l][None, :]
        sbq, sbk, sbv, rq, rk, rv, rg, gates = _inproj(
            x2d, bf(w_in), vec(b_gate), cos_t, sin_t, S)
        seq3 = lambda a: a.reshape(B, S, a.shape[-1])
        attn = _sb_attention(seq3(sbq), seq3(sbk), seq3(sbv), cum, cum3)
        retg = _retention(seq3(rq), seq3(rk), seq3(rv), seq3(rg), ret_tables)
        x1 = _mix_ln1(attn.reshape(B * S, SB_WIDTH), retg.reshape(B * S, RET_V_WIDTH), gates, x2d,
                      bf(w_sb_o), bf(w_ret_o), bf(w_mix_o), vec(ln1_g), vec(ln1_b))
        mk, mv = _mem_kv(mem, bf(w_mem_kv))
        x2 = _xattn_ln2(x1.reshape(B, S, D), mk, mv, bf(w_mem_q), bf(w_mem_o),
                        vec(ln2_g), vec(ln2_b))
        x2d = _ffn_ln3(x2.reshape(B * S, D), bf(w_ffn_in), bf(w_ffn_out), vec(ln3_g), vec(ln3_b))
    return x2d.reshape(B, S, D)
```

```python
import functools

import jax
import jax.numpy as jnp
import numpy as np
from jax import lax
from jax.experimental import pallas as pl
from jax.experimental.pallas import tpu as pltpu

F32 = jnp.float32
BF16 = jnp.bfloat16

D_MODEL = 1024
DEPTH = 1
SB_HEADS = 8
SB_HEAD_DIM = 64
SB_WIDTH = SB_HEADS * SB_HEAD_DIM
RET_HEADS = 4
RET_QK_DIM = 128
RET_V_DIM = 256
RET_QK_WIDTH = RET_HEADS * RET_QK_DIM
RET_V_WIDTH = RET_HEADS * RET_V_DIM
ROPE_BASE = 10000.0
N_BRANCHES = 2
OFF_SB_Q = 0
OFF_SB_K = OFF_SB_Q + SB_WIDTH
OFF_SB_V = OFF_SB_K + SB_WIDTH
OFF_RET_Q = OFF_SB_V + SB_WIDTH
OFF_RET_K = OFF_RET_Q + RET_QK_WIDTH
OFF_RET_V = OFF_RET_K + RET_QK_WIDTH
OFF_RET_G = OFF_RET_V + RET_V_WIDTH
OFF_GATE = OFF_RET_G + RET_V_WIDTH
IN_WIDTH = OFF_GATE + N_BRANCHES * D_MODEL
MEM_HEADS = 4
MEM_HEAD_DIM = D_MODEL // MEM_HEADS
FFN_HIDDEN = 2816
DN_ALPHA = (2.0 * DEPTH) ** 0.25
LN_EPS = 1e-5

LANES = 128
VMEM_LIMIT_BYTES = 56 * 1024 * 1024
ROW_TILE = 512
SUB_ROWS = 512
DENSE_SUBTILES = 2
SB_TILE = 128
SB_QTILES = 8
SB_TOP_ROWS = 32
SB_FIRST_STEPS = 3
SB_STEPS_PER_CHECK = 2
SB_UNDERFLOW = -150.5
LOG2_E = 1.4426950408889634
RET_CHUNK = 256
RET_ROWS = 4096
FFN_CHUNK = 256
FFN_ROWS = 1024


def _resident(shape):
    zeros = (0,) * len(shape)
    return pl.BlockSpec(shape, lambda *_: zeros, pipeline_mode=pl.Buffered(1))


def _params(*sem):
    return pltpu.CompilerParams(dimension_semantics=sem, vmem_limit_bytes=VMEM_LIMIT_BYTES)


def _residual_norm(x, branch_over_alpha, g, b):
    y = x + branch_over_alpha
    mu = jnp.mean(y, axis=-1, keepdims=True)
    yc = y - mu
    var = jnp.mean(yc * yc, axis=-1, keepdims=True)
    return yc * lax.rsqrt(var + LN_EPS / (DN_ALPHA * DN_ALPHA)) * g + b


def _inproj_kernel(x_ref, w_ref, bg_ref, cos_ref, sin_ref,
                   sbq_ref, sbk_ref, sbv_ref, rq_ref, rk_ref, rv_ref, rg_ref, gate_ref):
    xb = x_ref[...].astype(BF16)

    def proj(off, width):
        return jnp.dot(xb, w_ref[:, off:off + width], preferred_element_type=F32)

    for br in range(N_BRANCHES):
        sl = slice(br * D_MODEL, (br + 1) * D_MODEL)
        hg = proj(OFF_GATE + br * D_MODEL, D_MODEL) + bg_ref[:, sl]
        gate_ref[:, sl] = jax.nn.sigmoid(hg).astype(BF16)
    g = proj(OFF_RET_G, RET_V_WIDTH)
    rg_ref[...] = (g * jax.nn.sigmoid(g)).astype(BF16)

    def store_head_split(h, out_ref):
        hb = h.astype(BF16)
        first = lax.broadcasted_iota(jnp.int32, (hb.shape[0], LANES), 1) < SB_HEAD_DIM
        zero = jnp.zeros((hb.shape[0], LANES), BF16)
        for p in range(SB_WIDTH // LANES):
            pair = hb[:, p * LANES:(p + 1) * LANES]
            out_ref[:, 2 * p * LANES:(2 * p + 1) * LANES] = jnp.where(first, pair, zero)
            out_ref[:, (2 * p + 1) * LANES:(2 * p + 2) * LANES] = jnp.where(first, zero, pair)

    store_head_split(proj(OFF_SB_K, SB_WIDTH), sbk_ref)
    store_head_split(proj(OFF_SB_V, SB_WIDTH), sbv_ref)

    cos = cos_ref[...]
    sin = sin_ref[...]

    def rope_store(h, out_ref, scale):
        for hd in range(RET_HEADS):
            sl = slice(hd * RET_QK_DIM, (hd + 1) * RET_QK_DIM)
            xh = h[:, sl]
            rot = xh * cos + pltpu.roll(xh, RET_QK_DIM // 2, 1) * sin
            if scale != 1.0:
                rot = rot * scale
            out_ref[:, sl] = rot.astype(BF16)

    rope_store(proj(OFF_RET_Q, RET_QK_WIDTH), rq_ref, RET_QK_DIM ** -0.5)
    rope_store(proj(OFF_RET_K, RET_QK_WIDTH), rk_ref, 1.0)
    sbq_ref[...] = (proj(OFF_SB_Q, SB_WIDTH) * (SB_HEAD_DIM ** -0.5 * LOG2_E)).astype(BF16)
    rv_ref[...] = proj(OFF_RET_V, RET_V_WIDTH).astype(BF16)


def _inproj(x2d, w_in, b_gate, cos_t, sin_t, seq):
    T = x2d.shape[0]
    tm = ROW_TILE
    tiles_per_seq = seq // tm
    row = lambda w: pl.BlockSpec((tm, w), lambda i: (i, 0))
    tab = pl.BlockSpec((tm, LANES), lambda i: (i % tiles_per_seq, 0))
    widths = (SB_WIDTH, 2 * SB_WIDTH, 2 * SB_WIDTH, RET_QK_WIDTH, RET_QK_WIDTH,
              RET_V_WIDTH, RET_V_WIDTH, N_BRANCHES * D_MODEL)
    return pl.pallas_call(
        _inproj_kernel,
        grid=(T // tm,),
        in_specs=[row(D_MODEL), _resident((D_MODEL, IN_WIDTH)),
                  _resident((1, N_BRANCHES * D_MODEL)), tab, tab],
        out_specs=[row(w) for w in widths],
        out_shape=[jax.ShapeDtypeStruct((T, w), BF16) for w in widths],
        compiler_params=_params("parallel"),
        name="inproj",
    )(x2d, w_in, b_gate, cos_t, sin_t)


def _sb_kernel(q_ref, k_ref, v_ref, cum_ref, cum3_ref, o_ref):
    blk = pl.program_id(2)
    t = SB_TILE
    cum = cum_ref[...]
    row = lax.broadcasted_iota(jnp.int32, (t, 2 * t), 0)
    col = lax.broadcasted_iota(jnp.int32, (t, 2 * t), 1)
    causal = jnp.where(col >= t, col - t, col) < row

    def stack(x):
        return jnp.concatenate([x[:, :LANES], x[:, LANES:]], axis=0)

    def kv(j):
        start = pl.multiple_of(j * t, t)
        return stack(k_ref[0, pl.ds(start, t), :]), stack(v_ref[0, pl.ds(start, t), :])

    def q_tile(u):
        return q_ref[0, u * t:(u + 1) * t, :]

    def scores(q_rows, kbd):
        return lax.dot_general(q_rows, kbd, (((1,), (1,)), ((), ())), preferred_element_type=F32)

    def logs(y, mask):
        log_beta = jnp.minimum(y, 0.0) - jnp.log2(1.0 + jnp.exp2(-jnp.abs(y)))
        log_rem = log_beta - y
        if mask is not None:
            log_rem = jnp.where(mask, log_rem, 0.0)
        hi = log_rem.astype(BF16)
        lo = (log_rem - hi.astype(F32)).astype(BF16)
        rows = [jnp.concatenate([hi[:, h * t:(h + 1) * t], lo[:, h * t:(h + 1) * t]], axis=1)
                for h in range(2)]
        return log_beta, jnp.concatenate(rows, axis=0)

    def suffix_sums(operands):
        sums = jnp.dot(jnp.concatenate(operands, axis=0), cum, preferred_element_type=F32)
        out = []
        for n in range(len(operands)):
            a = sums[2 * n * t:(2 * n + 1) * t]
            b = sums[(2 * n + 1) * t:(2 * n + 2) * t]
            out.append((jnp.concatenate([a[:, :t], b[:, :t]], axis=1),
                        jnp.concatenate([a[:, t:], b[:, t:]], axis=1)))
        return out

    base = blk * SB_QTILES

    def fast_logs(y, mask):
        log_beta = jnp.minimum(y, 0.0) - jnp.log2(1.0 + jnp.exp2(-jnp.abs(y)))
        log_rem = log_beta - y
        if mask is not None:
            log_rem = jnp.where(mask, log_rem, 0.0)
        return log_beta, log_rem.astype(BF16)

    def fast():
        top = SB_TOP_ROWS
        nrows = (t, t, top)
        offsets = range(1 - len(nrows), SB_QTILES)
        members = {d: [(d + st, st) for st in range(len(nrows)) if 0 <= d + st < SB_QTILES]
                   for d in offsets}
        y, values = {}, {}
        for d in offsets:
            j = base + d
            kbd, vbd = kv(jnp.maximum(j, 0))
            if d < 0:
                vbd = jnp.where(j >= 0, vbd, jnp.zeros_like(vbd))
            values[d] = vbd
            q_rows = [q_ref[0, u * t:u * t + nrows[st], :] for u, st in members[d]]
            stacked = scores(jnp.concatenate(q_rows, axis=0), kbd)
            off = 0
            for u, st in members[d]:
                y[(u, st)] = stacked[off:off + nrows[st]]
                off += nrows[st]
        lb, hi = {}, {}
        for st in range(len(nrows)):
            for u in range(SB_QTILES):
                lb[(u, st)], hi[(u, st)] = fast_logs(y[(u, st)], causal if st == 0 else None)
        cum3 = cum3_ref[...]

        def later(st):
            n = nrows[st]
            ops = [jnp.concatenate([hi[(u, st - i)][:n] for i in range(st + 1)], axis=1)
                   for u in range(SB_QTILES)]
            out = jnp.dot(jnp.concatenate(ops, axis=0), cum3[:2 * t * (st + 1)],
                          preferred_element_type=F32)
            return [out[u * n:(u + 1) * n] for u in range(SB_QTILES)]

        sums = [later(st) for st in range(len(nrows))]
        w = {}
        for st in range(len(nrows)):
            for u in range(SB_QTILES):
                e = jnp.exp2(lb[(u, st)] + sums[st][u])
                w[(u, st)] = (jnp.where(causal, e, 0.0) if st == 0 else e).astype(BF16)
        acc, acc_top = [None] * SB_QTILES, [None] * SB_QTILES
        for d in offsets:
            ctx = jnp.dot(jnp.concatenate([w[key] for key in members[d]], axis=0), values[d],
                          preferred_element_type=F32)
            off = 0
            for u, st in members[d]:
                part = ctx[off:off + nrows[st]]
                off += nrows[st]
                if st == len(nrows) - 1:
                    acc_top[u] = part
                else:
                    acc[u] = part if acc[u] is None else acc[u] + part
        first_cols = lambda a: jnp.maximum(a[:, 0:1], a[:, t:t + 1])
        bound = None
        for u in range(SB_QTILES):
            b = jnp.maximum(jnp.max(first_cols(sums[2][u])), jnp.max(first_cols(sums[1][u][top:])))
            bound = b if bound is None else jnp.maximum(bound, b)
        return bound, acc, acc_top

    def walk(s, nsteps, run, acc):
        from_diag = isinstance(s, int) and s == 0
        steps = range(nsteps)
        offsets = range(1 - nsteps, SB_QTILES)
        members = {d: [(u, st) for st in steps for u in range(SB_QTILES) if u - st == d]
                   for d in offsets}
        order = [(u, st) for st in steps for u in range(SB_QTILES)]
        is_diag = lambda st: from_diag and st == 0
        y, values = {}, {}
        for d in offsets:
            j = base + d - s
            kbd, vbd = kv(jnp.maximum(j, 0))
            if not (from_diag and d >= 0):
                vbd = jnp.where(j >= 0, vbd, jnp.zeros_like(vbd))
            values[d] = vbd
            stacked = scores(jnp.concatenate([q_tile(u) for u, _ in members[d]], axis=0), kbd)
            for n, key in enumerate(members[d]):
                y[key] = stacked[n * t:(n + 1) * t]
        parts = {key: logs(y[key], causal if is_diag(key[1]) else None) for key in order}
        sums = []
        for st in steps:
            sums.extend(suffix_sums([parts[key][1] for key in order if key[1] == st]))
        run, acc, w = list(run), list(acc), {}
        for key, (later, rowsum) in zip(order, sums):
            u = key[0]
            if is_diag(key[1]):
                w[key] = jnp.where(causal, jnp.exp2(parts[key][0] + later), 0.0).astype(BF16)
                run[u] = rowsum
            else:
                w[key] = jnp.exp2(parts[key][0] + later + run[u]).astype(BF16)
                run[u] = run[u] + rowsum
        for d in offsets:
            ctx = jnp.dot(jnp.concatenate([w[key] for key in members[d]], axis=0), values[d],
                          preferred_element_type=F32)
            for n, (u, _) in enumerate(members[d]):
                part = ctx[n * t:(n + 1) * t]
                acc[u] = part if acc[u] is None else acc[u] + part
        top = run[0]
        for u in range(1, SB_QTILES):
            top = jnp.maximum(top, run[u])
        return jnp.max(top), run, acc

    def general():
        top, run, acc = walk(0, SB_FIRST_STEPS, [None] * SB_QTILES, [None] * SB_QTILES)

        def live(carry):
            return jnp.logical_and(carry[1] > SB_UNDERFLOW, carry[0] <= base + SB_QTILES - 1)

        def body(carry):
            s = carry[0]
            top, run, acc = walk(s, SB_STEPS_PER_CHECK, carry[2:2 + SB_QTILES],
                                 carry[2 + SB_QTILES:])
            return (s + SB_STEPS_PER_CHECK, top, *run, *acc)

        out = lax.while_loop(live, body, (jnp.int32(SB_FIRST_STEPS), top, *run, *acc))
        for u in range(SB_QTILES):
            o_ref[0, u * t:(u + 1) * t, :] = out[2 + SB_QTILES + u].astype(BF16)

    bound, acc, acc_top = fast()
    finished = bound < SB_UNDERFLOW

    @pl.when(finished)
    def _():
        for u in range(SB_QTILES):
            lo_rows = u * t + SB_TOP_ROWS
            o_ref[0, u * t:lo_rows, :] = (acc[u][:SB_TOP_ROWS] + acc_top[u]).astype(BF16)
            o_ref[0, lo_rows:(u + 1) * t, :] = acc[u][SB_TOP_ROWS:].astype(BF16)

    @pl.when(jnp.logical_not(finished))
    def _():
        general()


def _sb_attention(q, k, v, cum, cum3):
    B, S, _ = q.shape
    rows = SB_TILE * SB_QTILES
    pairs = SB_WIDTH // LANES
    qspec = pl.BlockSpec((1, rows, LANES), lambda b, p, i: (b, i, p))
    kvspec = pl.BlockSpec((1, S, 2 * LANES), lambda b, p, i: (b, 0, p))
    return pl.pallas_call(
        _sb_kernel,
        grid=(B, pairs, S // rows),
        in_specs=[qspec, kvspec, kvspec, _resident(cum.shape), _resident(cum3.shape)],
        out_specs=qspec,
        out_shape=jax.ShapeDtypeStruct((B, S, SB_WIDTH), BF16),
        compiler_params=_params("parallel", "parallel", "arbitrary"),
        name="sb_attn",
    )(q, k, v, cum, cum3)


def _ret_kernel(q_ref, k_ref, v_ref, g_ref, decay_ref, qd_ref, kd_ref, cd_ref, o_ref, state_ref):
    @pl.when(pl.program_id(2) == 0)
    def _():
        state_ref[...] = jnp.zeros_like(state_ref)

    c = RET_CHUNK
    chunks = [slice(ci * c, (ci + 1) * c) for ci in range(RET_ROWS // c)]
    inner, chunk_kv = [], []
    for sl in chunks:
        q = q_ref[0, sl, :]
        k = k_ref[0, sl, :]
        v = v_ref[0, sl, :]
        s = lax.dot_general(q, k, (((1,), (1,)), ((), ())), preferred_element_type=F32) * decay_ref[0]
        inner.append(jnp.dot(s.astype(BF16), v, preferred_element_type=F32))
        kdec = (k.astype(F32) * kd_ref[0]).astype(BF16)
        chunk_kv.append(lax.dot_general(kdec, v, (((0,), (0,)), ((), ())),
                                        preferred_element_type=F32))
    st = state_ref[...]
    before = []
    for kv in chunk_kv:
        before.append(st.astype(BF16))
        st = st * cd_ref[0] + kv
    state_ref[...] = st
    for sl, part, state in zip(chunks, inner, before):
        o = part + jnp.dot(q_ref[0, sl, :], state, preferred_element_type=F32) * qd_ref[0]
        mu = jnp.mean(o, axis=-1, keepdims=True)
        oc = o - mu
        var = jnp.mean(oc * oc, axis=-1, keepdims=True)
        normed = oc * lax.rsqrt(var + LN_EPS)
        o_ref[0, sl, :] = normed.astype(BF16) * g_ref[0, sl, :]


def _retention(rq, rk, rv, rg, tables):
    B, S, _ = rq.shape
    c = RET_CHUNK
    qk = pl.BlockSpec((1, RET_ROWS, RET_QK_DIM), lambda b, h, r: (b, r, h))
    vv = pl.BlockSpec((1, RET_ROWS, RET_V_DIM), lambda b, h, r: (b, r, h))
    head = lambda rows, cols: pl.BlockSpec((1, rows, cols), lambda b, h, r: (h, 0, 0))
    return pl.pallas_call(
        _ret_kernel,
        grid=(B, RET_HEADS, S // RET_ROWS),
        in_specs=[qk, qk, vv, vv, head(c, c), head(c, RET_V_DIM), head(c, RET_QK_DIM),
                  head(RET_QK_DIM, RET_V_DIM)],
        out_specs=vv,
        out_shape=jax.ShapeDtypeStruct((B, S, RET_V_WIDTH), BF16),
        scratch_shapes=[pltpu.VMEM((RET_QK_DIM, RET_V_DIM), F32)],
        compiler_params=_params("parallel", "parallel", "arbitrary"),
        name="retention",
    )(rq, rk, rv, rg, *tables)


def _retention_tables():
    c = RET_CHUNK
    log_gamma = np.log1p(-np.exp2(-5.0 - np.arange(RET_HEADS, dtype=np.float64)))
    idx = np.arange(c, dtype=np.float64)
    rel = idx[:, None] - idx[None, :]
    lg = log_gamma[:, None, None]
    decay = np.where(rel[None] >= 0, np.exp(lg * np.maximum(rel, 0.0)[None]), 0.0)
    qd = np.broadcast_to(np.exp(lg * (idx + 1.0)[None, :, None]), (RET_HEADS, c, RET_V_DIM))
    kd = np.broadcast_to(np.exp(lg * (c - 1.0 - idx)[None, :, None]), (RET_HEADS, c, RET_QK_DIM))
    cd = np.broadcast_to(np.exp(lg * float(c)), (RET_HEADS, RET_QK_DIM, RET_V_DIM))
    return tuple(jnp.asarray(t, F32) for t in (decay, qd, kd, cd))


def _mix_kernel(a_ref, r_ref, gate_ref, x_ref, wsb_ref, wret_ref, wmix_ref, g_ref, b_ref, o_ref):
    for sub in range(DENSE_SUBTILES):
        rows = slice(sub * SUB_ROWS, (sub + 1) * SUB_ROWS)
        y_sb = jnp.dot(a_ref[rows, :], wsb_ref[...], preferred_element_type=F32)
        y_ret = jnp.dot(r_ref[rows, :], wret_ref[...], preferred_element_type=F32)
        merged = (gate_ref[rows, :D_MODEL].astype(F32) * y_sb
                  + gate_ref[rows, D_MODEL:].astype(F32) * y_ret)
        mix = jnp.dot(merged.astype(BF16), wmix_ref[...], preferred_element_type=F32)
        o_ref[rows, :] = _residual_norm(x_ref[rows, :], mix, g_ref[...], b_ref[...])


def _mix_ln1(attn, retg, gates, x2d, w_sb_o, w_ret_o, w_mix_o, g, b):
    T = x2d.shape[0]
    tm = SUB_ROWS * DENSE_SUBTILES
    row = lambda w: pl.BlockSpec((tm, w), lambda i: (i, 0))
    return pl.pallas_call(
        _mix_kernel,
        grid=(T // tm,),
        in_specs=[row(SB_WIDTH), row(RET_V_WIDTH), row(N_BRANCHES * D_MODEL), row(D_MODEL),
                  _resident(w_sb_o.shape), _resident(w_ret_o.shape), _resident(w_mix_o.shape),
                  _resident(g.shape), _resident(b.shape)],
        out_specs=row(D_MODEL),
        out_shape=jax.ShapeDtypeStruct((T, D_MODEL), F32),
        compiler_params=_params("parallel"),
        name="mix_ln1",
    )(attn, retg, gates, x2d, w_sb_o, w_ret_o, w_mix_o, g, b)


def _memkv_kernel(m_ref, w_ref, k_ref, v_ref):
    mb = m_ref[0].astype(BF16)
    k_ref[0] = jnp.dot(mb, w_ref[:, :D_MODEL], preferred_element_type=F32).astype(BF16)
    v_ref[0] = jnp.dot(mb, w_ref[:, D_MODEL:], preferred_element_type=F32).astype(BF16)


def _mem_kv(mem, w_kv):
    B, M, _ = mem.shape
    blk = pl.BlockSpec((1, M, D_MODEL), lambda b: (b, 0, 0))
    return pl.pallas_call(
        _memkv_kernel,
        grid=(B,),
        in_specs=[blk, _resident(w_kv.shape)],
        out_specs=[blk, blk],
        out_shape=[jax.ShapeDtypeStruct((B, M, D_MODEL), BF16)] * 2,
        compiler_params=_params("parallel"),
        name="mem_kv",
    )(mem, w_kv)


def _xattn_kernel(x_ref, k_ref, v_ref, wq_ref, wo_ref, g_ref, b_ref, o_ref, ctx_ref):
    for sub in range(DENSE_SUBTILES):
        rows = slice(sub * SUB_ROWS, (sub + 1) * SUB_ROWS)
        x = x_ref[0, rows, :]
        q = (jnp.dot(x.astype(BF16), wq_ref[...], preferred_element_type=F32)
             * (MEM_HEAD_DIM ** -0.5)).astype(BF16)
        for h in range(MEM_HEADS):
            sl = slice(h * MEM_HEAD_DIM, (h + 1) * MEM_HEAD_DIM)
            s = lax.dot_general(q[:, sl], k_ref[0, :, sl], (((1,), (1,)), ((), ())),
                                preferred_element_type=F32)
            p = jnp.exp(s - jnp.max(s, axis=-1, keepdims=True))
            denom = jnp.sum(p, axis=-1, keepdims=True)
            ctx = jnp.dot(p.astype(BF16), v_ref[0, :, sl], preferred_element_type=F32)
            ctx_ref[rows, sl] = (ctx / denom).astype(BF16)
        xa = jnp.dot(ctx_ref[rows, :], wo_ref[...], preferred_element_type=F32)
        o_ref[0, rows, :] = _residual_norm(x, xa, g_ref[...], b_ref[...])


def _xattn_ln2(x3d, mk, mv, w_q, w_o, g, b):
    B, S, _ = x3d.shape
    M = mk.shape[1]
    tm = SUB_ROWS * DENSE_SUBTILES
    row = pl.BlockSpec((1, tm, D_MODEL), lambda bi, i: (bi, i, 0))
    kv = pl.BlockSpec((1, M, D_MODEL), lambda bi, i: (bi, 0, 0))
    return pl.pallas_call(
        _xattn_kernel,
        grid=(B, S // tm),
        in_specs=[row, kv, kv, _resident(w_q.shape), _resident(w_o.shape),
                  _resident(g.shape), _resident(b.shape)],
        out_specs=row,
        out_shape=jax.ShapeDtypeStruct((B, S, D_MODEL), F32),
        scratch_shapes=[pltpu.VMEM((tm, D_MODEL), BF16)],
        compiler_params=_params("parallel", "arbitrary"),
        name="xattn_ln2",
    )(x3d, mk, mv, w_q, w_o, g, b)


def _ffn_kernel(x_ref, win_ref, wout_ref, g_ref, b_ref, o_ref, hid_ref):
    x = x_ref[...]
    xb = x.astype(BF16)
    for c in range(FFN_HIDDEN // FFN_CHUNK):
        lo = c * FFN_CHUNK
        a = jnp.dot(xb, win_ref[:, lo:lo + FFN_CHUNK], preferred_element_type=F32)
        gte = jnp.dot(xb, win_ref[:, FFN_HIDDEN + lo:FFN_HIDDEN + lo + FFN_CHUNK],
                      preferred_element_type=F32)
        hid_ref[:, lo:lo + FFN_CHUNK] = (a * jax.nn.sigmoid(a) * gte).astype(BF16)
    ff = jnp.dot(hid_ref[...], wout_ref[...], preferred_element_type=F32)
    o_ref[...] = _residual_norm(x, ff, g_ref[...], b_ref[...])


def _ffn_ln3(x2d, w_in, w_out, g, b):
    T = x2d.shape[0]
    tm = FFN_ROWS
    row = pl.BlockSpec((tm, D_MODEL), lambda i: (i, 0))
    return pl.pallas_call(
        _ffn_kernel,
        grid=(T // tm,),
        in_specs=[row, _resident(w_in.shape), _resident(w_out.shape),
                  _resident(g.shape), _resident(b.shape)],
        out_specs=row,
        out_shape=jax.ShapeDtypeStruct((T, D_MODEL), F32),
        scratch_shapes=[pltpu.VMEM((tm, FFN_HIDDEN), BF16)],
        compiler_params=_params("parallel"),
        name="ffn_ln3",
    )(x2d, w_in, w_out, g, b)


def _rope_tables(seq):
    half = RET_QK_DIM // 2
    inv = 1.0 / (ROPE_BASE ** (np.arange(half, dtype=np.float64) / half))
    ang = np.arange(seq, dtype=np.float64)[:, None] * inv[None, :]
    cos, sin = np.cos(ang), np.sin(ang)
    return (jnp.asarray(np.concatenate([cos, cos], axis=-1), F32),
            jnp.asarray(np.concatenate([-sin, sin], axis=-1), F32))


def _cumsum_matrices():
    t = SB_TILE
    r = np.arange(t)
    upper = (r[:, None] > r[None, :]).astype(np.float32)
    ones = np.ones((t, t), np.float32)
    zero = np.zeros((t, t), np.float32)
    half = np.concatenate([upper, ones], axis=1)
    cum = np.concatenate([half, half], axis=0)
    both = lambda m: np.concatenate([np.concatenate([m, zero], axis=1),
                                     np.concatenate([zero, m], axis=1)], axis=0)
    cum3 = np.concatenate([both(upper)] + [both(ones)] * 2, axis=0)
    return jnp.asarray(cum, BF16), jnp.asarray(cum3, BF16)


def kernel(x, mem, w_in, b_gate, w_sb_o, w_ret_o, w_mix_o, ln1_g, ln1_b, w_mem_q, w_mem_kv,
           w_mem_o, ln2_g, ln2_b, w_ffn_in, w_ffn_out, ln3_g, ln3_b):
    B, S, D = x.shape
    assert D == D_MODEL and w_in.shape == (DEPTH, D_MODEL, IN_WIDTH)
    assert S % RET_ROWS == 0 and (B * S) % ROW_TILE == 0 and S % ROW_TILE == 0
    cos_t, sin_t = _rope_tables(S)
    cum, cum3 = _cumsum_matrices()
    ret_tables = _retention_tables()
    x2d = x.reshape(B * S, D)
    for l in range(DEPTH):
        bf = lambda w: w[l].astype(BF16)
        bf_over_alpha = lambda w: (w[l] * (1.0 / DN_ALPHA)).astype(BF16)
        vec = lambda p: p[l][None, :]
        sbq, sbk, sbv, rq, rk, rv, rg, gates = _inproj(
            x2d, bf(w_in), vec(b_gate), cos_t, sin_t, S)
        seq3 = lambda a: a.reshape(B, S, a.shape[-1])
        attn = _sb_attention(seq3(sbq), seq3(sbk), seq3(sbv), cum, cum3)
        retg = _retention(seq3(rq), seq3(rk), seq3(rv), seq3(rg), ret_tables)
        x1 = _mix_ln1(attn.reshape(B * S, SB_WIDTH), retg.reshape(B * S, RET_V_WIDTH), gates, x2d,
                      bf(w_sb_o), bf(w_ret_o), bf_over_alpha(w_mix_o), vec(ln1_g), vec(ln1_b))
        mk, mv = _mem_kv(mem, bf(w_mem_kv))
        x2 = _xattn_ln2(x1.reshape(B, S, D), mk, mv, bf(w_mem_q), bf_over_alpha(w_mem_o),
                        vec(ln2_g), vec(ln2_b))
        x2d = _ffn_ln3(x2.reshape(B * S, D), bf(w_ffn_in), bf_over_alpha(w_ffn_out), vec(ln3_g), vec(ln3_b))
    return x2d.reshape(B, S, D)
```

```python
import functools

import jax
import jax.numpy as jnp
import numpy as np
from jax import lax
from jax.experimental import pallas as pl
from jax.experimental.pallas import tpu as pltpu

F32 = jnp.float32
BF16 = jnp.bfloat16

D_MODEL = 1024
DEPTH = 1
SB_HEADS = 8
SB_HEAD_DIM = 64
SB_WIDTH = SB_HEADS * SB_HEAD_DIM
RET_HEADS = 4
RET_QK_DIM = 128
RET_V_DIM = 256
RET_QK_WIDTH = RET_HEADS * RET_QK_DIM
RET_V_WIDTH = RET_HEADS * RET_V_DIM
ROPE_BASE = 10000.0
N_BRANCHES = 2
OFF_SB_Q = 0
OFF_SB_K = OFF_SB_Q + SB_WIDTH
OFF_SB_V = OFF_SB_K + SB_WIDTH
OFF_RET_Q = OFF_SB_V + SB_WIDTH
OFF_RET_K = OFF_RET_Q + RET_QK_WIDTH
OFF_RET_V = OFF_RET_K + RET_QK_WIDTH
OFF_RET_G = OFF_RET_V + RET_V_WIDTH
OFF_GATE = OFF_RET_G + RET_V_WIDTH
IN_WIDTH = OFF_GATE + N_BRANCHES * D_MODEL
MEM_HEADS = 4
MEM_HEAD_DIM = D_MODEL // MEM_HEADS
FFN_HIDDEN = 2816
DN_ALPHA = (2.0 * DEPTH) ** 0.25
LN_EPS = 1e-5

LANES = 128
VMEM_LIMIT_BYTES = 56 * 1024 * 1024
ROW_TILE = 512
SUB_ROWS = 256
DENSE_SUBTILES = 4
SB_TILE = 128
SB_QTILES = 8
SB_TOP_ROWS = 32
SB_FIRST_STEPS = 3
SB_STEPS_PER_CHECK = 2
SB_UNDERFLOW = -150.5
LOG2_E = 1.4426950408889634
RET_CHUNK = 256
RET_ROWS = 4096
FFN_CHUNK = 256
FFN_ROWS = 1024
FFN_SUB_ROWS = 256


def _resident(shape):
    zeros = (0,) * len(shape)
    return pl.BlockSpec(shape, lambda *_: zeros, pipeline_mode=pl.Buffered(1))


def _params(*sem):
    return pltpu.CompilerParams(dimension_semantics=sem, vmem_limit_bytes=VMEM_LIMIT_BYTES)


def _sigmoid(x):
    return 0.5 * jnp.tanh(0.5 * x) + 0.5


def _residual_norm(x, branch_over_alpha, g, b):
    y = x + branch_over_alpha
    mu = jnp.mean(y, axis=-1, keepdims=True)
    yc = y - mu
    var = jnp.mean(yc * yc, axis=-1, keepdims=True)
    return yc * lax.rsqrt(var + LN_EPS / (DN_ALPHA * DN_ALPHA)) * g + b


def _inproj_kernel(x_ref, w_ref, bg_ref, cos_ref, sin_ref,
                   sbq_ref, sbk_ref, sbv_ref, rq_ref, rk_ref, rv_ref, rg_ref, gate_ref):
    xb = x_ref[...].astype(BF16)

    def proj(off, width):
        return jnp.dot(xb, w_ref[:, off:off + width], preferred_element_type=F32)

    for br in range(N_BRANCHES):
        sl = slice(br * D_MODEL, (br + 1) * D_MODEL)
        hg = proj(OFF_GATE + br * D_MODEL, D_MODEL) + bg_ref[:, sl]
        gate_ref[:, sl] = _sigmoid(hg).astype(BF16)
    g = proj(OFF_RET_G, RET_V_WIDTH)
    rg_ref[...] = (g * _sigmoid(g)).astype(BF16)

    def store_head_split(h, out_ref):
        hb = h.astype(BF16)
        first = lax.broadcasted_iota(jnp.int32, (hb.shape[0], LANES), 1) < SB_HEAD_DIM
        zero = jnp.zeros((hb.shape[0], LANES), BF16)
        for p in range(SB_WIDTH // LANES):
            pair = hb[:, p * LANES:(p + 1) * LANES]
            out_ref[:, 2 * p * LANES:(2 * p + 1) * LANES] = jnp.where(first, pair, zero)
            out_ref[:, (2 * p + 1) * LANES:(2 * p + 2) * LANES] = jnp.where(first, zero, pair)

    store_head_split(proj(OFF_SB_K, SB_WIDTH), sbk_ref)
    store_head_split(proj(OFF_SB_V, SB_WIDTH), sbv_ref)

    cos = cos_ref[...]
    sin = sin_ref[...]

    def rope_store(h, out_ref, scale):
        for hd in range(RET_HEADS):
            sl = slice(hd * RET_QK_DIM, (hd + 1) * RET_QK_DIM)
            xh = h[:, sl]
            rot = xh * cos + pltpu.roll(xh, RET_QK_DIM // 2, 1) * sin
            if scale != 1.0:
                rot = rot * scale
            out_ref[:, sl] = rot.astype(BF16)

    rope_store(proj(OFF_RET_Q, RET_QK_WIDTH), rq_ref, RET_QK_DIM ** -0.5)
    rope_store(proj(OFF_RET_K, RET_QK_WIDTH), rk_ref, 1.0)
    sbq_ref[...] = (proj(OFF_SB_Q, SB_WIDTH) * (SB_HEAD_DIM ** -0.5 * LOG2_E)).astype(BF16)
    rv_ref[...] = proj(OFF_RET_V, RET_V_WIDTH).astype(BF16)


def _inproj(x2d, w_in, b_gate, cos_t, sin_t, seq):
    T = x2d.shape[0]
    tm = ROW_TILE
    tiles_per_seq = seq // tm
    row = lambda w: pl.BlockSpec((tm, w), lambda i: (i, 0))
    tab = pl.BlockSpec((tm, LANES), lambda i: (i % tiles_per_seq, 0))
    widths = (SB_WIDTH, 2 * SB_WIDTH, 2 * SB_WIDTH, RET_QK_WIDTH, RET_QK_WIDTH,
              RET_V_WIDTH, RET_V_WIDTH, N_BRANCHES * D_MODEL)
    return pl.pallas_call(
        _inproj_kernel,
        grid=(T // tm,),
        in_specs=[row(D_MODEL), _resident((D_MODEL, IN_WIDTH)),
                  _resident((1, N_BRANCHES * D_MODEL)), tab, tab],
        out_specs=[row(w) for w in widths],
        out_shape=[jax.ShapeDtypeStruct((T, w), BF16) for w in widths],
        compiler_params=_params("parallel"),
        name="inproj",
    )(x2d, w_in, b_gate, cos_t, sin_t)


def _sb_kernel(q_ref, k_ref, v_ref, cum_ref, cum3_ref, o_ref):
    blk = pl.program_id(2)
    t = SB_TILE
    cum = cum_ref[...]
    row = lax.broadcasted_iota(jnp.int32, (t, 2 * t), 0)
    col = lax.broadcasted_iota(jnp.int32, (t, 2 * t), 1)
    causal = jnp.where(col >= t, col - t, col) < row

    def stack(x):
        return jnp.concatenate([x[:, :LANES], x[:, LANES:]], axis=0)

    def kv(j):
        start = pl.multiple_of(j * t, t)
        return stack(k_ref[0, pl.ds(start, t), :]), stack(v_ref[0, pl.ds(start, t), :])

    def q_tile(u):
        return q_ref[0, u * t:(u + 1) * t, :]

    def scores(q_rows, kbd):
        return lax.dot_general(q_rows, kbd, (((1,), (1,)), ((), ())), preferred_element_type=F32)

    def logs(y, mask):
        log_beta = jnp.minimum(y, 0.0) - jnp.log2(1.0 + jnp.exp2(-jnp.abs(y)))
        log_rem = log_beta - y
        if mask is not None:
            log_rem = jnp.where(mask, log_rem, 0.0)
        hi = log_rem.astype(BF16)
        lo = (log_rem - hi.astype(F32)).astype(BF16)
        rows = [jnp.concatenate([hi[:, h * t:(h + 1) * t], lo[:, h * t:(h + 1) * t]], axis=1)
                for h in range(2)]
        return log_beta, jnp.concatenate(rows, axis=0)

    def suffix_sums(operands):
        sums = jnp.dot(jnp.concatenate(operands, axis=0), cum, preferred_element_type=F32)
        out = []
        for n in range(len(operands)):
            a = sums[2 * n * t:(2 * n + 1) * t]
            b = sums[(2 * n + 1) * t:(2 * n + 2) * t]
            out.append((jnp.concatenate([a[:, :t], b[:, :t]], axis=1),
                        jnp.concatenate([a[:, t:], b[:, t:]], axis=1)))
        return out

    base = blk * SB_QTILES

    def fast_logs(y, mask):
        log_beta = jnp.minimum(y, 0.0) - jnp.log2(1.0 + jnp.exp2(-jnp.abs(y)))
        log_rem = log_beta - y
        if mask is not None:
            log_rem = jnp.where(mask, log_rem, 0.0)
        return log_beta, log_rem.astype(BF16)

    def fast():
        top = SB_TOP_ROWS
        nrows = (t, t, top)
        offsets = range(1 - len(nrows), SB_QTILES)
        members = {d: [(d + st, st) for st in range(len(nrows)) if 0 <= d + st < SB_QTILES]
                   for d in offsets}
        y, values = {}, {}
        for d in offsets:
            j = base + d
            kbd, vbd = kv(jnp.maximum(j, 0))
            if d < 0:
                vbd = jnp.where(j >= 0, vbd, jnp.zeros_like(vbd))
            values[d] = vbd
            q_rows = [q_ref[0, u * t:u * t + nrows[st], :] for u, st in members[d]]
            stacked = scores(jnp.concatenate(q_rows, axis=0), kbd)
            off = 0
            for u, st in members[d]:
                y[(u, st)] = stacked[off:off + nrows[st]]
                off += nrows[st]
        lb, hi = {}, {}
        for st in range(len(nrows)):
            for u in range(SB_QTILES):
                lb[(u, st)], hi[(u, st)] = fast_logs(y[(u, st)], causal if st == 0 else None)
        cum3 = cum3_ref[...]

        def later(st):
            n = nrows[st]
            ops = [jnp.concatenate([hi[(u, st - i)][:n] for i in range(st + 1)], axis=1)
                   for u in range(SB_QTILES)]
            out = jnp.dot(jnp.concatenate(ops, axis=0), cum3[:2 * t * (st + 1)],
                          preferred_element_type=F32)
            return [out[u * n:(u + 1) * n] for u in range(SB_QTILES)]

        sums = [later(st) for st in range(len(nrows))]
        w = {}
        for st in range(len(nrows)):
            for u in range(SB_QTILES):
                e = jnp.exp2(lb[(u, st)] + sums[st][u])
                w[(u, st)] = (jnp.where(causal, e, 0.0) if st == 0 else e).astype(BF16)
        acc, acc_top = [None] * SB_QTILES, [None] * SB_QTILES
        for d in offsets:
            ctx = jnp.dot(jnp.concatenate([w[key] for key in members[d]], axis=0), values[d],
                          preferred_element_type=F32)
            off = 0
            for u, st in members[d]:
                part = ctx[off:off + nrows[st]]
                off += nrows[st]
                if st == len(nrows) - 1:
                    acc_top[u] = part
                else:
                    acc[u] = part if acc[u] is None else acc[u] + part
        first_cols = lambda a: jnp.maximum(a[:, 0:1], a[:, t:t + 1])
        bound = None
        for u in range(SB_QTILES):
            b = jnp.maximum(jnp.max(first_cols(sums[2][u])), jnp.max(first_cols(sums[1][u][top:])))
            bound = b if bound is None else jnp.maximum(bound, b)
        return bound, acc, acc_top

    def walk(s, nsteps, run, acc):
        from_diag = isinstance(s, int) and s == 0
        steps = range(nsteps)
        offsets = range(1 - nsteps, SB_QTILES)
        members = {d: [(u, st) for st in steps for u in range(SB_QTILES) if u - st == d]
                   for d in offsets}
        order = [(u, st) for st in steps for u in range(SB_QTILES)]
        is_diag = lambda st: from_diag and st == 0
        y, values = {}, {}
        for d in offsets:
            j = base + d - s
            kbd, vbd = kv(jnp.maximum(j, 0))
            if not (from_diag and d >= 0):
                vbd = jnp.where(j >= 0, vbd, jnp.zeros_like(vbd))
            values[d] = vbd
            stacked = scores(jnp.concatenate([q_tile(u) for u, _ in members[d]], axis=0), kbd)
            for n, key in enumerate(members[d]):
                y[key] = stacked[n * t:(n + 1) * t]
        parts = {key: logs(y[key], causal if is_diag(key[1]) else None) for key in order}
        sums = []
        for st in steps:
            sums.extend(suffix_sums([parts[key][1] for key in order if key[1] == st]))
        run, acc, w = list(run), list(acc), {}
        for key, (later, rowsum) in zip(order, sums):
            u = key[0]
            if is_diag(key[1]):
                w[key] = jnp.where(causal, jnp.exp2(parts[key][0] + later), 0.0).astype(BF16)
                run[u] = rowsum
            else:
                w[key] = jnp.exp2(parts[key][0] + later + run[u]).astype(BF16)
                run[u] = run[u] + rowsum
        for d in offsets:
            ctx = jnp.dot(jnp.concatenate([w[key] for key in members[d]], axis=0), values[d],
                          preferred_element_type=F32)
            for n, (u, _) in enumerate(members[d]):
                part = ctx[n * t:(n + 1) * t]
                acc[u] = part if acc[u] is None else acc[u] + part
        top = run[0]
        for u in range(1, SB_QTILES):
            top = jnp.maximum(top, run[u])
        return jnp.max(top), run, acc

    def general():
        top, run, acc = walk(0, SB_FIRST_STEPS, [None] * SB_QTILES, [None] * SB_QTILES)

        def live(carry):
            return jnp.logical_and(carry[1] > SB_UNDERFLOW, carry[0] <= base + SB_QTILES - 1)

        def body(carry):
            s = carry[0]
            top, run, acc = walk(s, SB_STEPS_PER_CHECK, carry[2:2 + SB_QTILES],
                                 carry[2 + SB_QTILES:])
            return (s + SB_STEPS_PER_CHECK, top, *run, *acc)

        out = lax.while_loop(live, body, (jnp.int32(SB_FIRST_STEPS), top, *run, *acc))
        for u in range(SB_QTILES):
            o_ref[0, u * t:(u + 1) * t, :] = out[2 + SB_QTILES + u].astype(BF16)

    bound, acc, acc_top = fast()
    finished = bound < SB_UNDERFLOW

    @pl.when(finished)
    def _():
        for u in range(SB_QTILES):
            lo_rows = u * t + SB_TOP_ROWS
            o_ref[0, u * t:lo_rows, :] = (acc[u][:SB_TOP_ROWS] + acc_top[u]).astype(BF16)
            o_ref[0, lo_rows:(u + 1) * t, :] = acc[u][SB_TOP_ROWS:].astype(BF16)

    @pl.when(jnp.logical_not(finished))
    def _():
        general()


def _sb_attention(q, k, v, cum, cum3):
    B, S, _ = q.shape
    rows = SB_TILE * SB_QTILES
    pairs = SB_WIDTH // LANES
    qspec = pl.BlockSpec((1, rows, LANES), lambda b, p, i: (b, i, p))
    kvspec = pl.BlockSpec((1, S, 2 * LANES), lambda b, p, i: (b, 0, p))
    return pl.pallas_call(
        _sb_kernel,
        grid=(B, pairs, S // rows),
        in_specs=[qspec, kvspec, kvspec, _resident(cum.shape), _resident(cum3.shape)],
        out_specs=qspec,
        out_shape=jax.ShapeDtypeStruct((B, S, SB_WIDTH), BF16),
        compiler_params=_params("parallel", "parallel", "arbitrary"),
        name="sb_attn",
    )(q, k, v, cum, cum3)


def _ret_kernel(q_ref, k_ref, v_ref, g_ref, decay_ref, qd_ref, kd_ref, cd_ref, o_ref, state_ref):
    @pl.when(pl.program_id(2) == 0)
    def _():
        state_ref[...] = jnp.zeros_like(state_ref)

    c = RET_CHUNK
    chunks = [slice(ci * c, (ci + 1) * c) for ci in range(RET_ROWS // c)]
    inner, chunk_kv = [], []
    for sl in chunks:
        q = q_ref[0, sl, :]
        k = k_ref[0, sl, :]
        v = v_ref[0, sl, :]
        s = lax.dot_general(q, k, (((1,), (1,)), ((), ())), preferred_element_type=F32) * decay_ref[0]
        inner.append(jnp.dot(s.astype(BF16), v, preferred_element_type=F32))
        kdec = (k.astype(F32) * kd_ref[0]).astype(BF16)
        chunk_kv.append(lax.dot_general(kdec, v, (((0,), (0,)), ((), ())),
                                        preferred_element_type=F32))
    st = state_ref[...]
    before = []
    for kv in chunk_kv:
        before.append(st.astype(BF16))
        st = st * cd_ref[0] + kv
    state_ref[...] = st
    for sl, part, state in zip(chunks, inner, before):
        o = part + jnp.dot(q_ref[0, sl, :], state, preferred_element_type=F32) * qd_ref[0]
        mu = jnp.mean(o, axis=-1, keepdims=True)
        oc = o - mu
        var = jnp.mean(oc * oc, axis=-1, keepdims=True)
        normed = oc * lax.rsqrt(var + LN_EPS)
        o_ref[0, sl, :] = normed.astype(BF16) * g_ref[0, sl, :]


def _retention(rq, rk, rv, rg, tables):
    B, S, _ = rq.shape
    c = RET_CHUNK
    qk = pl.BlockSpec((1, RET_ROWS, RET_QK_DIM), lambda b, h, r: (b, r, h))
    vv = pl.BlockSpec((1, RET_ROWS, RET_V_DIM), lambda b, h, r: (b, r, h))
    head = lambda rows, cols: pl.BlockSpec((1, rows, cols), lambda b, h, r: (h, 0, 0))
    return pl.pallas_call(
        _ret_kernel,
        grid=(B, RET_HEADS, S // RET_ROWS),
        in_specs=[qk, qk, vv, vv, head(c, c), head(c, RET_V_DIM), head(c, RET_QK_DIM),
                  head(RET_QK_DIM, RET_V_DIM)],
        out_specs=vv,
        out_shape=jax.ShapeDtypeStruct((B, S, RET_V_WIDTH), BF16),
        scratch_shapes=[pltpu.VMEM((RET_QK_DIM, RET_V_DIM), F32)],
        compiler_params=_params("parallel", "parallel", "arbitrary"),
        name="retention",
    )(rq, rk, rv, rg, *tables)


def _retention_tables():
    c = RET_CHUNK
    log_gamma = np.log1p(-np.exp2(-5.0 - np.arange(RET_HEADS, dtype=np.float64)))
    idx = np.arange(c, dtype=np.float64)
    rel = idx[:, None] - idx[None, :]
    lg = log_gamma[:, None, None]
    decay = np.where(rel[None] >= 0, np.exp(lg * np.maximum(rel, 0.0)[None]), 0.0)
    qd = np.broadcast_to(np.exp(lg * (idx + 1.0)[None, :, None]), (RET_HEADS, c, RET_V_DIM))
    kd = np.broadcast_to(np.exp(lg * (c - 1.0 - idx)[None, :, None]), (RET_HEADS, c, RET_QK_DIM))
    cd = np.broadcast_to(np.exp(lg * float(c)), (RET_HEADS, RET_QK_DIM, RET_V_DIM))
    return tuple(jnp.asarray(t, F32) for t in (decay, qd, kd, cd))


def _mix_kernel(a_ref, r_ref, gate_ref, x_ref, wsb_ref, wret_ref, wmix_ref, g_ref, b_ref, o_ref):
    for sub in range(DENSE_SUBTILES):
        rows = slice(sub * SUB_ROWS, (sub + 1) * SUB_ROWS)
        y_sb = jnp.dot(a_ref[rows, :], wsb_ref[...], preferred_element_type=F32)
        y_ret = jnp.dot(r_ref[rows, :], wret_ref[...], preferred_element_type=F32)
        merged = (gate_ref[rows, :D_MODEL] * y_sb.astype(BF16)
                  + gate_ref[rows, D_MODEL:] * y_ret.astype(BF16))
        mix = jnp.dot(merged, wmix_ref[...], preferred_element_type=F32)
        o_ref[rows, :] = _residual_norm(x_ref[rows, :], mix, g_ref[...], b_ref[...])


def _mix_ln1(attn, retg, gates, x2d, w_sb_o, w_ret_o, w_mix_o, g, b):
    T = x2d.shape[0]
    tm = SUB_ROWS * DENSE_SUBTILES
    row = lambda w: pl.BlockSpec((tm, w), lambda i: (i, 0))
    return pl.pallas_call(
        _mix_kernel,
        grid=(T // tm,),
        in_specs=[row(SB_WIDTH), row(RET_V_WIDTH), row(N_BRANCHES * D_MODEL), row(D_MODEL),
                  _resident(w_sb_o.shape), _resident(w_ret_o.shape), _resident(w_mix_o.shape),
                  _resident(g.shape), _resident(b.shape)],
        out_specs=row(D_MODEL),
        out_shape=jax.ShapeDtypeStruct((T, D_MODEL), F32),
        compiler_params=_params("parallel"),
        name="mix_ln1",
    )(attn, retg, gates, x2d, w_sb_o, w_ret_o, w_mix_o, g, b)


def _memkv_kernel(m_ref, w_ref, k_ref, v_ref):
    mb = m_ref[0].astype(BF16)
    k_ref[0] = jnp.dot(mb, w_ref[:, :D_MODEL], preferred_element_type=F32).astype(BF16)
    v_ref[0] = jnp.dot(mb, w_ref[:, D_MODEL:], preferred_element_type=F32).astype(BF16)


def _mem_kv(mem, w_kv):
    B, M, _ = mem.shape
    blk = pl.BlockSpec((1, M, D_MODEL), lambda b: (b, 0, 0))
    return pl.pallas_call(
        _memkv_kernel,
        grid=(B,),
        in_specs=[blk, _resident(w_kv.shape)],
        out_specs=[blk, blk],
        out_shape=[jax.ShapeDtypeStruct((B, M, D_MODEL), BF16)] * 2,
        compiler_params=_params("parallel"),
        name="mem_kv",
    )(mem, w_kv)


def _xattn_kernel(x_ref, k_ref, v_ref, wq_ref, wo_ref, g_ref, b_ref, o_ref, ctx_ref):
    for sub in range(DENSE_SUBTILES):
        rows = slice(sub * SUB_ROWS, (sub + 1) * SUB_ROWS)
        x = x_ref[0, rows, :]
        q = jnp.dot(x.astype(BF16), wq_ref[...], preferred_element_type=F32).astype(BF16)
        for h in range(MEM_HEADS):
            sl = slice(h * MEM_HEAD_DIM, (h + 1) * MEM_HEAD_DIM)
            s = lax.dot_general(q[:, sl], k_ref[0, :, sl], (((1,), (1,)), ((), ())),
                                preferred_element_type=F32)
            p = jnp.exp2(s - jnp.max(s, axis=-1, keepdims=True))
            denom = jnp.sum(p, axis=-1, keepdims=True)
            ctx = jnp.dot(p.astype(BF16), v_ref[0, :, sl], preferred_element_type=F32)
            ctx_ref[rows, sl] = (ctx / denom).astype(BF16)
        xa = jnp.dot(ctx_ref[rows, :], wo_ref[...], preferred_element_type=F32)
        o_ref[0, rows, :] = _residual_norm(x, xa, g_ref[...], b_ref[...])


def _xattn_ln2(x3d, mk, mv, w_q, w_o, g, b):
    B, S, _ = x3d.shape
    M = mk.shape[1]
    tm = SUB_ROWS * DENSE_SUBTILES
    row = pl.BlockSpec((1, tm, D_MODEL), lambda bi, i: (bi, i, 0))
    kv = pl.BlockSpec((1, M, D_MODEL), lambda bi, i: (bi, 0, 0))
    return pl.pallas_call(
        _xattn_kernel,
        grid=(B, S // tm),
        in_specs=[row, kv, kv, _resident(w_q.shape), _resident(w_o.shape),
                  _resident(g.shape), _resident(b.shape)],
        out_specs=row,
        out_shape=jax.ShapeDtypeStruct((B, S, D_MODEL), F32),
        scratch_shapes=[pltpu.VMEM((tm, D_MODEL), BF16)],
        compiler_params=_params("parallel", "arbitrary"),
        name="xattn_ln2",
    )(x3d, mk, mv, w_q, w_o, g, b)


def _ffn_kernel(x_ref, win_ref, wout_ref, g_ref, b_ref, o_ref, hid_ref):
    for sub in range(FFN_ROWS // FFN_SUB_ROWS):
        rows = slice(sub * FFN_SUB_ROWS, (sub + 1) * FFN_SUB_ROWS)
        x = x_ref[rows, :]
        xb = x.astype(BF16)
        for c in range(FFN_HIDDEN // FFN_CHUNK):
            lo = c * FFN_CHUNK
            a = jnp.dot(xb, win_ref[:, lo:lo + FFN_CHUNK], preferred_element_type=F32)
            gte = jnp.dot(xb, win_ref[:, FFN_HIDDEN + lo:FFN_HIDDEN + lo + FFN_CHUNK],
                          preferred_element_type=F32)
            hid_ref[rows, lo:lo + FFN_CHUNK] = (a * _sigmoid(a) * gte).astype(BF16)
        ff = jnp.dot(hid_ref[rows, :], wout_ref[...], preferred_element_type=F32)
        o_ref[rows, :] = _residual_norm(x, ff, g_ref[...], b_ref[...])


def _ffn_ln3(x2d, w_in, w_out, g, b):
    T = x2d.shape[0]
    tm = FFN_ROWS
    row = pl.BlockSpec((tm, D_MODEL), lambda i: (i, 0))
    return pl.pallas_call(
        _ffn_kernel,
        grid=(T // tm,),
        in_specs=[row, _resident(w_in.shape), _resident(w_out.shape),
                  _resident(g.shape), _resident(b.shape)],
        out_specs=row,
        out_shape=jax.ShapeDtypeStruct((T, D_MODEL), F32),
        scratch_shapes=[pltpu.VMEM((tm, FFN_HIDDEN), BF16)],
        compiler_params=_params("parallel"),
        name="ffn_ln3",
    )(x2d, w_in, w_out, g, b)


def _rope_tables(seq):
    half = RET_QK_DIM // 2
    inv = 1.0 / (ROPE_BASE ** (np.arange(half, dtype=np.float64) / half))
    ang = np.arange(seq, dtype=np.float64)[:, None] * inv[None, :]
    cos, sin = np.cos(ang), np.sin(ang)
    return (jnp.asarray(np.concatenate([cos, cos], axis=-1), F32),
            jnp.asarray(np.concatenate([-sin, sin], axis=-1), F32))


def _cumsum_matrices():
    t = SB_TILE
    r = np.arange(t)
    upper = (r[:, None] > r[None, :]).astype(np.float32)
    ones = np.ones((t, t), np.float32)
    zero = np.zeros((t, t), np.float32)
    half = np.concatenate([upper, ones], axis=1)
    cum = np.concatenate([half, half], axis=0)
    both = lambda m: np.concatenate([np.concatenate([m, zero], axis=1),
                                     np.concatenate([zero, m], axis=1)], axis=0)
    cum3 = np.concatenate([both(upper)] + [both(ones)] * 2, axis=0)
    return jnp.asarray(cum, BF16), jnp.asarray(cum3, BF16)


def kernel(x, mem, w_in, b_gate, w_sb_o, w_ret_o, w_mix_o, ln1_g, ln1_b, w_mem_q, w_mem_kv,
           w_mem_o, ln2_g, ln2_b, w_ffn_in, w_ffn_out, ln3_g, ln3_b):
    B, S, D = x.shape
    assert D == D_MODEL and w_in.shape == (DEPTH, D_MODEL, IN_WIDTH)
    assert S % RET_ROWS == 0 and (B * S) % ROW_TILE == 0 and S % ROW_TILE == 0
    cos_t, sin_t = _rope_tables(S)
    cum, cum3 = _cumsum_matrices()
    ret_tables = _retention_tables()
    x2d = x.reshape(B * S, D)
    for l in range(DEPTH):
        bf = lambda w: w[l].astype(BF16)
        bf_over_alpha = lambda w: (w[l] * (1.0 / DN_ALPHA)).astype(BF16)
        vec = lambda p: p[l][None, :]
        sbq, sbk, sbv, rq, rk, rv, rg, gates = _inproj(
            x2d, bf(w_in), vec(b_gate), cos_t, sin_t, S)
        seq3 = lambda a: a.reshape(B, S, a.shape[-1])
        attn = _sb_attention(seq3(sbq), seq3(sbk), seq3(sbv), cum, cum3)
        retg = _retention(seq3(rq), seq3(rk), seq3(rv), seq3(rg), ret_tables)
        x1 = _mix_ln1(attn.reshape(B * S, SB_WIDTH), retg.reshape(B * S, RET_V_WIDTH), gates, x2d,
                      bf(w_sb_o), bf(w_ret_o), bf_over_alpha(w_mix_o), vec(ln1_g), vec(ln1_b))
        mk, mv = _mem_kv(mem, bf(w_mem_kv))
        w_q_scaled = (w_mem_q[l] * (MEM_HEAD_DIM ** -0.5 * LOG2_E)).astype(BF16)
        x2 = _xattn_ln2(x1.reshape(B, S, D), mk, mv, w_q_scaled, bf_over_alpha(w_mem_o),
                        vec(ln2_g), vec(ln2_b))
        x2d = _ffn_ln3(x2.reshape(B * S, D), bf(w_ffn_in), bf_over_alpha(w_ffn_out), vec(ln3_g), vec(ln3_b))
    return x2d.reshape(B, S, D)
```

```python
import functools

import jax
import jax.numpy as jnp
import numpy as np
from jax import lax
from jax.experimental import pallas as pl
from jax.experimental.pallas import tpu as pltpu

F32 = jnp.float32
BF16 = jnp.bfloat16

D_MODEL = 1024
DEPTH = 1
SB_HEADS = 8
SB_HEAD_DIM = 64
SB_WIDTH = SB_HEADS * SB_HEAD_DIM
RET_HEADS = 4
RET_QK_DIM = 128
RET_V_DIM = 256
RET_QK_WIDTH = RET_HEADS * RET_QK_DIM
RET_V_WIDTH = RET_HEADS * RET_V_DIM
ROPE_BASE = 10000.0
N_BRANCHES = 2
OFF_SB_Q = 0
OFF_SB_K = OFF_SB_Q + SB_WIDTH
OFF_SB_V = OFF_SB_K + SB_WIDTH
OFF_RET_Q = OFF_SB_V + SB_WIDTH
OFF_RET_K = OFF_RET_Q + RET_QK_WIDTH
OFF_RET_V = OFF_RET_K + RET_QK_WIDTH
OFF_RET_G = OFF_RET_V + RET_V_WIDTH
OFF_GATE = OFF_RET_G + RET_V_WIDTH
IN_WIDTH = OFF_GATE + N_BRANCHES * D_MODEL
MEM_HEADS = 4
MEM_HEAD_DIM = D_MODEL // MEM_HEADS
FFN_HIDDEN = 2816
DN_ALPHA = (2.0 * DEPTH) ** 0.25
LN_EPS = 1e-5

LANES = 128
VMEM_LIMIT_BYTES = 56 * 1024 * 1024
ROW_TILE = 512
SUB_ROWS = 256
DENSE_SUBTILES = 4
SB_TILE = 128
SB_QTILES = 16
SB_TOP_ROWS = 32
SB_FIRST_STEPS = 3
SB_STEPS_PER_CHECK = 2
SB_UNDERFLOW = -150.5
LOG2_E = 1.4426950408889634
RET_CHUNK = 256
RET_ROWS = 4096
FFN_CHUNK = 256
FFN_ROWS = 1024
FFN_SUB_ROWS = 256


def _resident(shape):
    zeros = (0,) * len(shape)
    return pl.BlockSpec(shape, lambda *_: zeros, pipeline_mode=pl.Buffered(1))


def _params(*sem):
    return pltpu.CompilerParams(dimension_semantics=sem, vmem_limit_bytes=VMEM_LIMIT_BYTES)


def _sigmoid(x):
    return 0.5 * jnp.tanh(0.5 * x) + 0.5


def _residual_norm(x, branch_over_alpha, g, b):
    y = x + branch_over_alpha
    mu = jnp.mean(y, axis=-1, keepdims=True)
    yc = y - mu
    var = jnp.mean(yc * yc, axis=-1, keepdims=True)
    return yc * lax.rsqrt(var + LN_EPS / (DN_ALPHA * DN_ALPHA)) * g + b


def _inproj_kernel(x_ref, w_ref, bg_ref, cos_ref, sin_ref,
                   sbq_ref, sbk_ref, sbv_ref, rq_ref, rk_ref, rv_ref, rg_ref, gate_ref):
    xb = x_ref[...].astype(BF16)

    def proj(off, width):
        return jnp.dot(xb, w_ref[:, off:off + width], preferred_element_type=F32)

    for br in range(N_BRANCHES):
        sl = slice(br * D_MODEL, (br + 1) * D_MODEL)
        hg = proj(OFF_GATE + br * D_MODEL, D_MODEL) + bg_ref[:, sl]
        gate_ref[:, sl] = _sigmoid(hg).astype(BF16)
    g = proj(OFF_RET_G, RET_V_WIDTH)
    rg_ref[...] = (g * _sigmoid(g)).astype(BF16)

    def store_head_split(h, out_ref):
        hb = h.astype(BF16)
        first = lax.broadcasted_iota(jnp.int32, (hb.shape[0], LANES), 1) < SB_HEAD_DIM
        zero = jnp.zeros((hb.shape[0], LANES), BF16)
        for p in range(SB_WIDTH // LANES):
            pair = hb[:, p * LANES:(p + 1) * LANES]
            out_ref[:, 2 * p * LANES:(2 * p + 1) * LANES] = jnp.where(first, pair, zero)
            out_ref[:, (2 * p + 1) * LANES:(2 * p + 2) * LANES] = jnp.where(first, zero, pair)

    store_head_split(proj(OFF_SB_K, SB_WIDTH), sbk_ref)
    store_head_split(proj(OFF_SB_V, SB_WIDTH), sbv_ref)

    cos = cos_ref[...]
    sin = sin_ref[...]

    def rope_store(h, out_ref, scale):
        for hd in range(RET_HEADS):
            sl = slice(hd * RET_QK_DIM, (hd + 1) * RET_QK_DIM)
            xh = h[:, sl]
            rot = xh * cos + pltpu.roll(xh, RET_QK_DIM // 2, 1) * sin
            if scale != 1.0:
                rot = rot * scale
            out_ref[:, sl] = rot.astype(BF16)

    rope_store(proj(OFF_RET_Q, RET_QK_WIDTH), rq_ref, RET_QK_DIM ** -0.5)
    rope_store(proj(OFF_RET_K, RET_QK_WIDTH), rk_ref, 1.0)
    sbq_ref[...] = (proj(OFF_SB_Q, SB_WIDTH) * (SB_HEAD_DIM ** -0.5 * LOG2_E)).astype(BF16)
    rv_ref[...] = proj(OFF_RET_V, RET_V_WIDTH).astype(BF16)


def _inproj(x2d, w_in, b_gate, cos_t, sin_t, seq):
    T = x2d.shape[0]
    tm = ROW_TILE
    tiles_per_seq = seq // tm
    row = lambda w: pl.BlockSpec((tm, w), lambda i: (i, 0))
    tab = pl.BlockSpec((tm, LANES), lambda i: (i % tiles_per_seq, 0))
    widths = (SB_WIDTH, 2 * SB_WIDTH, 2 * SB_WIDTH, RET_QK_WIDTH, RET_QK_WIDTH,
              RET_V_WIDTH, RET_V_WIDTH, N_BRANCHES * D_MODEL)
    return pl.pallas_call(
        _inproj_kernel,
        grid=(T // tm,),
        in_specs=[row(D_MODEL), _resident((D_MODEL, IN_WIDTH)),
                  _resident((1, N_BRANCHES * D_MODEL)), tab, tab],
        out_specs=[row(w) for w in widths],
        out_shape=[jax.ShapeDtypeStruct((T, w), BF16) for w in widths],
        compiler_params=_params("parallel"),
        name="inproj",
    )(x2d, w_in, b_gate, cos_t, sin_t)


def _sb_kernel(q_ref, k_ref, v_ref, cum_ref, cum3_ref, o_ref):
    blk = pl.program_id(2)
    t = SB_TILE
    cum = cum_ref[...]
    row = lax.broadcasted_iota(jnp.int32, (t, 2 * t), 0)
    col = lax.broadcasted_iota(jnp.int32, (t, 2 * t), 1)
    causal = jnp.where(col >= t, col - t, col) < row

    def stack(x):
        return jnp.concatenate([x[:, :LANES], x[:, LANES:]], axis=0)

    def kv(j):
        start = pl.multiple_of(j * t, t)
        return stack(k_ref[0, pl.ds(start, t), :]), stack(v_ref[0, pl.ds(start, t), :])

    def q_tile(u):
        return q_ref[0, u * t:(u + 1) * t, :]

    def scores(q_rows, kbd):
        return lax.dot_general(q_rows, kbd, (((1,), (1,)), ((), ())), preferred_element_type=F32)

    def logs(y, mask):
        log_beta = jnp.minimum(y, 0.0) - jnp.log2(1.0 + jnp.exp2(-jnp.abs(y)))
        log_rem = log_beta - y
        if mask is not None:
            log_rem = jnp.where(mask, log_rem, 0.0)
        hi = log_rem.astype(BF16)
        lo = (log_rem - hi.astype(F32)).astype(BF16)
        rows = [jnp.concatenate([hi[:, h * t:(h + 1) * t], lo[:, h * t:(h + 1) * t]], axis=1)
                for h in range(2)]
        return log_beta, jnp.concatenate(rows, axis=0)

    def suffix_sums(operands):
        sums = jnp.dot(jnp.concatenate(operands, axis=0), cum, preferred_element_type=F32)
        out = []
        for n in range(len(operands)):
            a = sums[2 * n * t:(2 * n + 1) * t]
            b = sums[(2 * n + 1) * t:(2 * n + 2) * t]
            out.append((jnp.concatenate([a[:, :t], b[:, :t]], axis=1),
                        jnp.concatenate([a[:, t:], b[:, t:]], axis=1)))
        return out

    base = blk * SB_QTILES

    def fast_logs(y, mask):
        log_beta = jnp.minimum(y, 0.0) - jnp.log2(1.0 + jnp.exp2(-jnp.abs(y)))
        log_rem = log_beta - y
        if mask is not None:
            log_rem = jnp.where(mask, log_rem, 0.0)
        return log_beta, log_rem.astype(BF16)

    def fast():
        top = SB_TOP_ROWS
        nrows = (t, t, top)
        offsets = range(1 - len(nrows), SB_QTILES)
        members = {d: [(d + st, st) for st in range(len(nrows)) if 0 <= d + st < SB_QTILES]
                   for d in offsets}
        y, values = {}, {}
        for d in offsets:
            j = base + d
            kbd, vbd = kv(jnp.maximum(j, 0))
            if d < 0:
                vbd = jnp.where(j >= 0, vbd, jnp.zeros_like(vbd))
            values[d] = vbd
            q_rows = [q_ref[0, u * t:u * t + nrows[st], :] for u, st in members[d]]
            stacked = scores(jnp.concatenate(q_rows, axis=0), kbd)
            off = 0
            for u, st in members[d]:
                y[(u, st)] = stacked[off:off + nrows[st]]
                off += nrows[st]
        lb, hi = {}, {}
        for st in range(len(nrows)):
            for u in range(SB_QTILES):
                lb[(u, st)], hi[(u, st)] = fast_logs(y[(u, st)], causal if st == 0 else None)
        cum3 = cum3_ref[...]

        def later(st):
            n = nrows[st]
            ops = [jnp.concatenate([hi[(u, st - i)][:n] for i in range(st + 1)], axis=1)
                   for u in range(SB_QTILES)]
            out = jnp.dot(jnp.concatenate(ops, axis=0), cum3[:2 * t * (st + 1)],
                          preferred_element_type=F32)
            return [out[u * n:(u + 1) * n] for u in range(SB_QTILES)]

        sums = [later(st) for st in range(len(nrows))]
        w = {}
        for st in range(len(nrows)):
            for u in range(SB_QTILES):
                e = jnp.exp2(lb[(u, st)] + sums[st][u])
                w[(u, st)] = (jnp.where(causal, e, 0.0) if st == 0 else e).astype(BF16)
        acc, acc_top = [None] * SB_QTILES, [None] * SB_QTILES
        for d in offsets:
            ctx = jnp.dot(jnp.concatenate([w[key] for key in members[d]], axis=0), values[d],
                          preferred_element_type=F32)
            off = 0
            for u, st in members[d]:
                part = ctx[off:off + nrows[st]]
                off += nrows[st]
                if st == len(nrows) - 1:
                    acc_top[u] = part
                else:
                    acc[u] = part if acc[u] is None else acc[u] + part
        first_cols = lambda a: jnp.maximum(a[:, 0:1], a[:, t:t + 1])
        bound = None
        for u in range(SB_QTILES):
            b = jnp.maximum(jnp.max(first_cols(sums[2][u])), jnp.max(first_cols(sums[1][u][top:])))
            bound = b if bound is None else jnp.maximum(bound, b)
        return bound, acc, acc_top

    def walk(s, nsteps, run, acc):
        from_diag = isinstance(s, int) and s == 0
        steps = range(nsteps)
        offsets = range(1 - nsteps, SB_QTILES)
        members = {d: [(u, st) for st in steps for u in range(SB_QTILES) if u - st == d]
                   for d in offsets}
        order = [(u, st) for st in steps for u in range(SB_QTILES)]
        is_diag = lambda st: from_diag and st == 0
        y, values = {}, {}
        for d in offsets:
            j = base + d - s
            kbd, vbd = kv(jnp.maximum(j, 0))
            if not (from_diag and d >= 0):
                vbd = jnp.where(j >= 0, vbd, jnp.zeros_like(vbd))
            values[d] = vbd
            stacked = scores(jnp.concatenate([q_tile(u) for u, _ in members[d]], axis=0), kbd)
            for n, key in enumerate(members[d]):
                y[key] = stacked[n * t:(n + 1) * t]
        parts = {key: logs(y[key], causal if is_diag(key[1]) else None) for key in order}
        sums = []
        for st in steps:
            sums.extend(suffix_sums([parts[key][1] for key in order if key[1] == st]))
        run, acc, w = list(run), list(acc), {}
        for key, (later, rowsum) in zip(order, sums):
            u = key[0]
            if is_diag(key[1]):
                w[key] = jnp.where(causal, jnp.exp2(parts[key][0] + later), 0.0).astype(BF16)
                run[u] = rowsum
            else:
                w[key] = jnp.exp2(parts[key][0] + later + run[u]).astype(BF16)
                run[u] = run[u] + rowsum
        for d in offsets:
            ctx = jnp.dot(jnp.concatenate([w[key] for key in members[d]], axis=0), values[d],
                          preferred_element_type=F32)
            for n, (u, _) in enumerate(members[d]):
                part = ctx[n * t:(n + 1) * t]
                acc[u] = part if acc[u] is None else acc[u] + part
        top = run[0]
        for u in range(1, SB_QTILES):
            top = jnp.maximum(top, run[u])
        return jnp.max(top), run, acc

    def general():
        top, run, acc = walk(0, SB_FIRST_STEPS, [None] * SB_QTILES, [None] * SB_QTILES)

        def live(carry):
            return jnp.logical_and(carry[1] > SB_UNDERFLOW, carry[0] <= base + SB_QTILES - 1)

        def body(carry):
            s = carry[0]
            top, run, acc = walk(s, SB_STEPS_PER_CHECK, carry[2:2 + SB_QTILES],
                                 carry[2 + SB_QTILES:])
            return (s + SB_STEPS_PER_CHECK, top, *run, *acc)

        out = lax.while_loop(live, body, (jnp.int32(SB_FIRST_STEPS), top, *run, *acc))
        for u in range(SB_QTILES):
            o_ref[0, u * t:(u + 1) * t, :] = out[2 + SB_QTILES + u].astype(BF16)

    bound, acc, acc_top = fast()
    finished = bound < SB_UNDERFLOW

    @pl.when(finished)
    def _():
        for u in range(SB_QTILES):
            lo_rows = u * t + SB_TOP_ROWS
            o_ref[0, u * t:lo_rows, :] = (acc[u][:SB_TOP_ROWS] + acc_top[u]).astype(BF16)
            o_ref[0, lo_rows:(u + 1) * t, :] = acc[u][SB_TOP_ROWS:].astype(BF16)

    @pl.when(jnp.logical_not(finished))
    def _():
        general()


def _sb_attention(q, k, v, cum, cum3):
    B, S, _ = q.shape
    rows = SB_TILE * SB_QTILES
    pairs = SB_WIDTH // LANES
    qspec = pl.BlockSpec((1, rows, LANES), lambda b, p, i: (b, i, p))
    kvspec = pl.BlockSpec((1, S, 2 * LANES), lambda b, p, i: (b, 0, p))
    return pl.pallas_call(
        _sb_kernel,
        grid=(B, pairs, S // rows),
        in_specs=[qspec, kvspec, kvspec, _resident(cum.shape), _resident(cum3.shape)],
        out_specs=qspec,
        out_shape=jax.ShapeDtypeStruct((B, S, SB_WIDTH), BF16),
        compiler_params=_params("parallel", "parallel", "arbitrary"),
        name="sb_attn",
    )(q, k, v, cum, cum3)


def _ret_kernel(q_ref, k_ref, v_ref, g_ref, decay_ref, qd_ref, kd_ref, cd_ref, o_ref, state_ref):
    @pl.when(pl.program_id(2) == 0)
    def _():
        state_ref[...] = jnp.zeros_like(state_ref)

    c = RET_CHUNK
    chunks = [slice(ci * c, (ci + 1) * c) for ci in range(RET_ROWS // c)]
    inner, chunk_kv = [], []
    for sl in chunks:
        q = q_ref[0, sl, :]
        k = k_ref[0, sl, :]
        v = v_ref[0, sl, :]
        s = lax.dot_general(q, k, (((1,), (1,)), ((), ())), preferred_element_type=F32) * decay_ref[0]
        inner.append(jnp.dot(s.astype(BF16), v, preferred_element_type=F32))
        kdec = (k.astype(F32) * kd_ref[0]).astype(BF16)
        chunk_kv.append(lax.dot_general(kdec, v, (((0,), (0,)), ((), ())),
                                        preferred_element_type=F32))
    st = state_ref[...]
    before = []
    for kv in chunk_kv:
        before.append(st.astype(BF16))
        st = st * cd_ref[0] + kv
    state_ref[...] = st
    for sl, part, state in zip(chunks, inner, before):
        o = part + jnp.dot(q_ref[0, sl, :], state, preferred_element_type=F32) * qd_ref[0]
        mu = jnp.mean(o, axis=-1, keepdims=True)
        oc = o - mu
        var = jnp.mean(oc * oc, axis=-1, keepdims=True)
        normed = oc * lax.rsqrt(var + LN_EPS)
        o_ref[0, sl, :] = normed.astype(BF16) * g_ref[0, sl, :]


def _retention(rq, rk, rv, rg, tables):
    B, S, _ = rq.shape
    c = RET_CHUNK
    qk = pl.BlockSpec((1, RET_ROWS, RET_QK_DIM), lambda b, h, r: (b, r, h))
    vv = pl.BlockSpec((1, RET_ROWS, RET_V_DIM), lambda b, h, r: (b, r, h))
    head = lambda rows, cols: pl.BlockSpec((1, rows, cols), lambda b, h, r: (h, 0, 0))
    return pl.pallas_call(
        _ret_kernel,
        grid=(B, RET_HEADS, S // RET_ROWS),
        in_specs=[qk, qk, vv, vv, head(c, c), head(c, RET_V_DIM), head(c, RET_QK_DIM),
                  head(RET_QK_DIM, RET_V_DIM)],
        out_specs=vv,
        out_shape=jax.ShapeDtypeStruct((B, S, RET_V_WIDTH), BF16),
        scratch_shapes=[pltpu.VMEM((RET_QK_DIM, RET_V_DIM), F32)],
        compiler_params=_params("parallel", "parallel", "arbitrary"),
        name="retention",
    )(rq, rk, rv, rg, *tables)


def _retention_tables():
    c = RET_CHUNK
    log_gamma = np.log1p(-np.exp2(-5.0 - np.arange(RET_HEADS, dtype=np.float64)))
    idx = np.arange(c, dtype=np.float64)
    rel = idx[:, None] - idx[None, :]
    lg = log_gamma[:, None, None]
    decay = np.where(rel[None] >= 0, np.exp(lg * np.maximum(rel, 0.0)[None]), 0.0)
    qd = np.broadcast_to(np.exp(lg * (idx + 1.0)[None, :, None]), (RET_HEADS, c, RET_V_DIM))
    kd = np.broadcast_to(np.exp(lg * (c - 1.0 - idx)[None, :, None]), (RET_HEADS, c, RET_QK_DIM))
    cd = np.broadcast_to(np.exp(lg * float(c)), (RET_HEADS, RET_QK_DIM, RET_V_DIM))
    return tuple(jnp.asarray(t, F32) for t in (decay, qd, kd, cd))


def _mix_kernel(a_ref, r_ref, gate_ref, x_ref, wsb_ref, wret_ref, wmix_ref, g_ref, b_ref, o_ref):
    for sub in range(DENSE_SUBTILES):
        rows = slice(sub * SUB_ROWS, (sub + 1) * SUB_ROWS)
        y_sb = jnp.dot(a_ref[rows, :], wsb_ref[...], preferred_element_type=F32)
        y_ret = jnp.dot(r_ref[rows, :], wret_ref[...], preferred_element_type=F32)
        merged = (gate_ref[rows, :D_MODEL] * y_sb.astype(BF16)
                  + gate_ref[rows, D_MODEL:] * y_ret.astype(BF16))
        mix = jnp.dot(merged, wmix_ref[...], preferred_element_type=F32)
        o_ref[rows, :] = _residual_norm(x_ref[rows, :], mix, g_ref[...], b_ref[...])


def _mix_ln1(attn, retg, gates, x2d, w_sb_o, w_ret_o, w_mix_o, g, b):
    T = x2d.shape[0]
    tm = SUB_ROWS * DENSE_SUBTILES
    row = lambda w: pl.BlockSpec((tm, w), lambda i: (i, 0))
    return pl.pallas_call(
        _mix_kernel,
        grid=(T // tm,),
        in_specs=[row(SB_WIDTH), row(RET_V_WIDTH), row(N_BRANCHES * D_MODEL), row(D_MODEL),
                  _resident(w_sb_o.shape), _resident(w_ret_o.shape), _resident(w_mix_o.shape),
                  _resident(g.shape), _resident(b.shape)],
        out_specs=row(D_MODEL),
        out_shape=jax.ShapeDtypeStruct((T, D_MODEL), F32),
        compiler_params=_params("parallel"),
        name="mix_ln1",
    )(attn, retg, gates, x2d, w_sb_o, w_ret_o, w_mix_o, g, b)


def _memkv_kernel(m_ref, w_ref, k_ref, v_ref):
    mb = m_ref[0].astype(BF16)
    k_ref[0] = jnp.dot(mb, w_ref[:, :D_MODEL], preferred_element_type=F32).astype(BF16)
    v_ref[0] = jnp.dot(mb, w_ref[:, D_MODEL:], preferred_element_type=F32).astype(BF16)


def _mem_kv(mem, w_kv):
    B, M, _ = mem.shape
    blk = pl.BlockSpec((1, M, D_MODEL), lambda b: (b, 0, 0))
    return pl.pallas_call(
        _memkv_kernel,
        grid=(B,),
        in_specs=[blk, _resident(w_kv.shape)],
        out_specs=[blk, blk],
        out_shape=[jax.ShapeDtypeStruct((B, M, D_MODEL), BF16)] * 2,
        compiler_params=_params("parallel"),
        name="mem_kv",
    )(mem, w_kv)


def _xattn_kernel(x_ref, k_ref, v_ref, wq_ref, wo_ref, g_ref, b_ref, o_ref, ctx_ref):
    for sub in range(DENSE_SUBTILES):
        rows = slice(sub * SUB_ROWS, (sub + 1) * SUB_ROWS)
        x = x_ref[0, rows, :]
        q = jnp.dot(x.astype(BF16), wq_ref[...], preferred_element_type=F32).astype(BF16)
        for h in range(MEM_HEADS):
            sl = slice(h * MEM_HEAD_DIM, (h + 1) * MEM_HEAD_DIM)
            s = lax.dot_general(q[:, sl], k_ref[0, :, sl], (((1,), (1,)), ((), ())),
                                preferred_element_type=F32)
            p = jnp.exp2(s - jnp.max(s, axis=-1, keepdims=True))
            denom = jnp.sum(p, axis=-1, keepdims=True)
            ctx = jnp.dot(p.astype(BF16), v_ref[0, :, sl], preferred_element_type=F32)
            ctx_ref[rows, sl] = (ctx / denom).astype(BF16)
        xa = jnp.dot(ctx_ref[rows, :], wo_ref[...], preferred_element_type=F32)
        o_ref[0, rows, :] = _residual_norm(x, xa, g_ref[...], b_ref[...])


def _xattn_ln2(x3d, mk, mv, w_q, w_o, g, b):
    B, S, _ = x3d.shape
    M = mk.shape[1]
    tm = SUB_ROWS * DENSE_SUBTILES
    row = pl.BlockSpec((1, tm, D_MODEL), lambda bi, i: (bi, i, 0))
    kv = pl.BlockSpec((1, M, D_MODEL), lambda bi, i: (bi, 0, 0))
    return pl.pallas_call(
        _xattn_kernel,
        grid=(B, S // tm),
        in_specs=[row, kv, kv, _resident(w_q.shape), _resident(w_o.shape),
                  _resident(g.shape), _resident(b.shape)],
        out_specs=row,
        out_shape=jax.ShapeDtypeStruct((B, S, D_MODEL), F32),
        scratch_shapes=[pltpu.VMEM((tm, D_MODEL), BF16)],
        compiler_params=_params("parallel", "arbitrary"),
        name="xattn_ln2",
    )(x3d, mk, mv, w_q, w_o, g, b)


def _ffn_kernel(x_ref, win_ref, wout_ref, g_ref, b_ref, o_ref, hid_ref):
    for sub in range(FFN_ROWS // FFN_SUB_ROWS):
        rows = slice(sub * FFN_SUB_ROWS, (sub + 1) * FFN_SUB_ROWS)
        x = x_ref[rows, :]
        xb = x.astype(BF16)
        for c in range(FFN_HIDDEN // FFN_CHUNK):
            lo = c * FFN_CHUNK
            a = jnp.dot(xb, win_ref[:, lo:lo + FFN_CHUNK], preferred_element_type=F32)
            gte = jnp.dot(xb, win_ref[:, FFN_HIDDEN + lo:FFN_HIDDEN + lo + FFN_CHUNK],
                          preferred_element_type=F32)
            hid_ref[rows, lo:lo + FFN_CHUNK] = (a * _sigmoid(a) * gte).astype(BF16)
        ff = jnp.dot(hid_ref[rows, :], wout_ref[...], preferred_element_type=F32)
        o_ref[rows, :] = _residual_norm(x, ff, g_ref[...], b_ref[...])


def _ffn_ln3(x2d, w_in, w_out, g, b):
    T = x2d.shape[0]
    tm = FFN_ROWS
    row = pl.BlockSpec((tm, D_MODEL), lambda i: (i, 0))
    return pl.pallas_call(
        _ffn_kernel,
        grid=(T // tm,),
        in_specs=[row, _resident(w_in.shape), _resident(w_out.shape),
                  _resident(g.shape), _resident(b.shape)],
        out_specs=row,
        out_shape=jax.ShapeDtypeStruct((T, D_MODEL), F32),
        scratch_shapes=[pltpu.VMEM((tm, FFN_HIDDEN), BF16)],
        compiler_params=_params("parallel"),
        name="ffn_ln3",
    )(x2d, w_in, w_out, g, b)


def _rope_tables(seq):
    half = RET_QK_DIM // 2
    inv = 1.0 / (ROPE_BASE ** (np.arange(half, dtype=np.float64) / half))
    ang = np.arange(seq, dtype=np.float64)[:, None] * inv[None, :]
    cos, sin = np.cos(ang), np.sin(ang)
    return (jnp.asarray(np.concatenate([cos, cos], axis=-1), F32),
            jnp.asarray(np.concatenate([-sin, sin], axis=-1), F32))


def _cumsum_matrices():
    t = SB_TILE
    r = np.arange(t)
    upper = (r[:, None] > r[None, :]).astype(np.float32)
    ones = np.ones((t, t), np.float32)
    zero = np.zeros((t, t), np.float32)
    half = np.concatenate([upper, ones], axis=1)
    cum = np.concatenate([half, half], axis=0)
    both = lambda m: np.concatenate([np.concatenate([m, zero], axis=1),
                                     np.concatenate([zero, m], axis=1)], axis=0)
    cum3 = np.concatenate([both(upper)] + [both(ones)] * 2, axis=0)
    return jnp.asarray(cum, BF16), jnp.asarray(cum3, BF16)


def kernel(x, mem, w_in, b_gate, w_sb_o, w_ret_o, w_mix_o, ln1_g, ln1_b, w_mem_q, w_mem_kv,
           w_mem_o, ln2_g, ln2_b, w_ffn_in, w_ffn_out, ln3_g, ln3_b):
    B, S, D = x.shape
    assert D == D_MODEL and w_in.shape == (DEPTH, D_MODEL, IN_WIDTH)
    assert S % RET_ROWS == 0 and (B * S) % ROW_TILE == 0 and S % ROW_TILE == 0
    cos_t, sin_t = _rope_tables(S)
    cum, cum3 = _cumsum_matrices()
    ret_tables = _retention_tables()
    x2d = x.reshape(B * S, D)
    for l in range(DEPTH):
        bf = lambda w: w[l].astype(BF16)
        bf_over_alpha = lambda w: (w[l] * (1.0 / DN_ALPHA)).astype(BF16)
        vec = lambda p: p[l][None, :]
        sbq, sbk, sbv, rq, rk, rv, rg, gates = _inproj(
            x2d, bf(w_in), vec(b_gate), cos_t, sin_t, S)
        seq3 = lambda a: a.reshape(B, S, a.shape[-1])
        attn = _sb_attention(seq3(sbq), seq3(sbk), seq3(sbv), cum, cum3)
        retg = _retention(seq3(rq), seq3(rk), seq3(rv), seq3(rg), ret_tables)
        x1 = _mix_ln1(attn.reshape(B * S, SB_WIDTH), retg.reshape(B * S, RET_V_WIDTH), gates, x2d,
                      bf(w_sb_o), bf(w_ret_o), bf_over_alpha(w_mix_o), vec(ln1_g), vec(ln1_b))
        mk, mv = _mem_kv(mem, bf(w_mem_kv))
        w_q_scaled = (w_mem_q[l] * (MEM_HEAD_DIM ** -0.5 * LOG2_E)).astype(BF16)
        x2 = _xattn_ln2(x1.reshape(B, S, D), mk, mv, w_q_scaled, bf_over_alpha(w_mem_o),
                        vec(ln2_g), vec(ln2_b))
        x2d = _ffn_ln3(x2.reshape(B * S, D), bf(w_ffn_in), bf_over_alpha(w_ffn_out), vec(ln3_g), vec(ln3_b))
    return x2d.reshape(B, S, D)
```

```python
import functools

import jax
import jax.numpy as jnp
import numpy as np
from jax import lax
from jax.experimental import pallas as pl
from jax.experimental.pallas import tpu as pltpu

F32 = jnp.float32
BF16 = jnp.bfloat16

D_MODEL = 1024
DEPTH = 1
SB_HEADS = 8
SB_HEAD_DIM = 64
SB_WIDTH = SB_HEADS * SB_HEAD_DIM
RET_HEADS = 4
RET_QK_DIM = 128
RET_V_DIM = 256
RET_QK_WIDTH = RET_HEADS * RET_QK_DIM
RET_V_WIDTH = RET_HEADS * RET_V_DIM
ROPE_BASE = 10000.0
N_BRANCHES = 2
OFF_SB_Q = 0
OFF_SB_K = OFF_SB_Q + SB_WIDTH
OFF_SB_V = OFF_SB_K + SB_WIDTH
OFF_RET_Q = OFF_SB_V + SB_WIDTH
OFF_RET_K = OFF_RET_Q + RET_QK_WIDTH
OFF_RET_V = OFF_RET_K + RET_QK_WIDTH
OFF_RET_G = OFF_RET_V + RET_V_WIDTH
OFF_GATE = OFF_RET_G + RET_V_WIDTH
IN_WIDTH = OFF_GATE + N_BRANCHES * D_MODEL
MEM_HEADS = 4
MEM_HEAD_DIM = D_MODEL // MEM_HEADS
FFN_HIDDEN = 2816
DN_ALPHA = (2.0 * DEPTH) ** 0.25
LN_EPS = 1e-5

LANES = 128
VMEM_LIMIT_BYTES = 56 * 1024 * 1024
ROW_TILE = 512
SUB_ROWS = 256
DENSE_SUBTILES = 4
SB_TILE = 128
SB_QTILES = 16
SB_GROUP = 8
SB_TOP_ROWS = 32
SB_FIRST_STEPS = 3
SB_STEPS_PER_CHECK = 2
SB_UNDERFLOW = -150.5
LOG2_E = 1.4426950408889634
RET_CHUNK = 256
RET_ROWS = 4096
FFN_CHUNK = 256
FFN_ROWS = 1024
FFN_SUB_ROWS = 256


def _resident(shape):
    zeros = (0,) * len(shape)
    return pl.BlockSpec(shape, lambda *_: zeros, pipeline_mode=pl.Buffered(1))


def _params(*sem):
    return pltpu.CompilerParams(dimension_semantics=sem, vmem_limit_bytes=VMEM_LIMIT_BYTES)


def _sigmoid(x):
    return 0.5 * jnp.tanh(0.5 * x) + 0.5


def _residual_norm(x, branch_over_alpha, g, b):
    y = x + branch_over_alpha
    mu = jnp.mean(y, axis=-1, keepdims=True)
    yc = y - mu
    var = jnp.mean(yc * yc, axis=-1, keepdims=True)
    return yc * lax.rsqrt(var + LN_EPS / (DN_ALPHA * DN_ALPHA)) * g + b


def _inproj_kernel(x_ref, w_ref, bg_ref, cos_ref, sin_ref,
                   sbq_ref, sbk_ref, sbv_ref, rq_ref, rk_ref, rv_ref, rg_ref, gate_ref):
    xb = x_ref[...].astype(BF16)

    def proj(off, width):
        return jnp.dot(xb, w_ref[:, off:off + width], preferred_element_type=F32)

    for br in range(N_BRANCHES):
        sl = slice(br * D_MODEL, (br + 1) * D_MODEL)
        hg = proj(OFF_GATE + br * D_MODEL, D_MODEL) + bg_ref[:, sl]
        gate_ref[:, sl] = _sigmoid(hg).astype(BF16)
    g = proj(OFF_RET_G, RET_V_WIDTH)
    rg_ref[...] = (g * _sigmoid(g)).astype(BF16)

    def store_head_split(h, out_ref):
        hb = h.astype(BF16)
        first = lax.broadcasted_iota(jnp.int32, (hb.shape[0], LANES), 1) < SB_HEAD_DIM
        zero = jnp.zeros((hb.shape[0], LANES), BF16)
        for p in range(SB_WIDTH // LANES):
            pair = hb[:, p * LANES:(p + 1) * LANES]
            out_ref[:, 2 * p * LANES:(2 * p + 1) * LANES] = jnp.where(first, pair, zero)
            out_ref[:, (2 * p + 1) * LANES:(2 * p + 2) * LANES] = jnp.where(first, zero, pair)

    store_head_split(proj(OFF_SB_K, SB_WIDTH), sbk_ref)
    store_head_split(proj(OFF_SB_V, SB_WIDTH), sbv_ref)

    cos = cos_ref[...]
    sin = sin_ref[...]

    def rope_store(h, out_ref, scale):
        for hd in range(RET_HEADS):
            sl = slice(hd * RET_QK_DIM, (hd + 1) * RET_QK_DIM)
            xh = h[:, sl]
            rot = xh * cos + pltpu.roll(xh, RET_QK_DIM // 2, 1) * sin
            if scale != 1.0:
                rot = rot * scale
            out_ref[:, sl] = rot.astype(BF16)

    rope_store(proj(OFF_RET_Q, RET_QK_WIDTH), rq_ref, RET_QK_DIM ** -0.5)
    rope_store(proj(OFF_RET_K, RET_QK_WIDTH), rk_ref, 1.0)
    sbq_ref[...] = (proj(OFF_SB_Q, SB_WIDTH) * (SB_HEAD_DIM ** -0.5 * LOG2_E)).astype(BF16)
    rv_ref[...] = proj(OFF_RET_V, RET_V_WIDTH).astype(BF16)


def _inproj(x2d, w_in, b_gate, cos_t, sin_t, seq):
    T = x2d.shape[0]
    tm = ROW_TILE
    tiles_per_seq = seq // tm
    row = lambda w: pl.BlockSpec((tm, w), lambda i: (i, 0))
    tab = pl.BlockSpec((tm, LANES), lambda i: (i % tiles_per_seq, 0))
    widths = (SB_WIDTH, 2 * SB_WIDTH, 2 * SB_WIDTH, RET_QK_WIDTH, RET_QK_WIDTH,
              RET_V_WIDTH, RET_V_WIDTH, N_BRANCHES * D_MODEL)
    return pl.pallas_call(
        _inproj_kernel,
        grid=(T // tm,),
        in_specs=[row(D_MODEL), _resident((D_MODEL, IN_WIDTH)),
                  _resident((1, N_BRANCHES * D_MODEL)), tab, tab],
        out_specs=[row(w) for w in widths],
        out_shape=[jax.ShapeDtypeStruct((T, w), BF16) for w in widths],
        compiler_params=_params("parallel"),
        name="inproj",
    )(x2d, w_in, b_gate, cos_t, sin_t)


def _sb_kernel(q_ref, k_ref, v_ref, cum_ref, cum3_ref, o_ref):
    blk = pl.program_id(2)
    t = SB_TILE
    cum = cum_ref[...]
    row = lax.broadcasted_iota(jnp.int32, (t, 2 * t), 0)
    col = lax.broadcasted_iota(jnp.int32, (t, 2 * t), 1)
    causal = jnp.where(col >= t, col - t, col) < row

    def stack(x):
        return jnp.concatenate([x[:, :LANES], x[:, LANES:]], axis=0)

    def kv(j):
        start = pl.multiple_of(j * t, t)
        return stack(k_ref[0, pl.ds(start, t), :]), stack(v_ref[0, pl.ds(start, t), :])

    def q_tile(u):
        return q_ref[0, u * t:(u + 1) * t, :]

    def scores(q_rows, kbd):
        return lax.dot_general(q_rows, kbd, (((1,), (1,)), ((), ())), preferred_element_type=F32)

    def logs(y, mask):
        log_beta = jnp.minimum(y, 0.0) - jnp.log2(1.0 + jnp.exp2(-jnp.abs(y)))
        log_rem = log_beta - y
        if mask is not None:
            log_rem = jnp.where(mask, log_rem, 0.0)
        hi = log_rem.astype(BF16)
        lo = (log_rem - hi.astype(F32)).astype(BF16)
        rows = [jnp.concatenate([hi[:, h * t:(h + 1) * t], lo[:, h * t:(h + 1) * t]], axis=1)
                for h in range(2)]
        return log_beta, jnp.concatenate(rows, axis=0)

    def suffix_sums(operands):
        sums = jnp.dot(jnp.concatenate(operands, axis=0), cum, preferred_element_type=F32)
        out = []
        for n in range(len(operands)):
            a = sums[2 * n * t:(2 * n + 1) * t]
            b = sums[(2 * n + 1) * t:(2 * n + 2) * t]
            out.append((jnp.concatenate([a[:, :t], b[:, :t]], axis=1),
                        jnp.concatenate([a[:, t:], b[:, t:]], axis=1)))
        return out

    base = blk * SB_QTILES

    def fast_logs(y, mask):
        log_beta = jnp.minimum(y, 0.0) - jnp.log2(1.0 + jnp.exp2(-jnp.abs(y)))
        log_rem = log_beta - y
        if mask is not None:
            log_rem = jnp.where(mask, log_rem, 0.0)
        return log_beta, log_rem.astype(BF16)

    def fast(chains):
        top = SB_TOP_ROWS
        nrows = (t, t, top)
        offsets = range(chains[0] + 1 - len(nrows), chains[-1] + 1)
        members = {d: [(d + st, st) for st in range(len(nrows)) if d + st in chains]
                   for d in offsets}
        y, values = {}, {}
        for d in offsets:
            j = base + d
            kbd, vbd = kv(jnp.maximum(j, 0))
            if d < 0:
                vbd = jnp.where(j >= 0, vbd, jnp.zeros_like(vbd))
            values[d] = vbd
            q_rows = [q_ref[0, u * t:u * t + nrows[st], :] for u, st in members[d]]
            stacked = scores(jnp.concatenate(q_rows, axis=0), kbd)
            off = 0
            for u, st in members[d]:
                y[(u, st)] = stacked[off:off + nrows[st]]
                off += nrows[st]
        lb, hi = {}, {}
        for st in range(len(nrows)):
            for u in chains:
                lb[(u, st)], hi[(u, st)] = fast_logs(y[(u, st)], causal if st == 0 else None)
        cum3 = cum3_ref[...]

        def later(st):
            n = nrows[st]
            ops = [jnp.concatenate([hi[(u, st - i)][:n] for i in range(st + 1)], axis=1)
                   for u in chains]
            out = jnp.dot(jnp.concatenate(ops, axis=0), cum3[:2 * t * (st + 1)],
                          preferred_element_type=F32)
            return {u: out[i * n:(i + 1) * n] for i, u in enumerate(chains)}

        sums = [later(st) for st in range(len(nrows))]
        w = {}
        for st in range(len(nrows)):
            for u in chains:
                e = jnp.exp2(lb[(u, st)] + sums[st][u])
                w[(u, st)] = (jnp.where(causal, e, 0.0) if st == 0 else e).astype(BF16)
        acc, acc_top = {u: None for u in chains}, {}
        for d in offsets:
            ctx = jnp.dot(jnp.concatenate([w[key] for key in members[d]], axis=0), values[d],
                          preferred_element_type=F32)
            off = 0
            for u, st in members[d]:
                part = ctx[off:off + nrows[st]]
                off += nrows[st]
                if st == len(nrows) - 1:
                    acc_top[u] = part
                else:
                    acc[u] = part if acc[u] is None else acc[u] + part
        first_cols = lambda a: jnp.maximum(a[:, 0:1], a[:, t:t + 1])
        bound = None
        for u in chains:
            b = jnp.maximum(jnp.max(first_cols(sums[2][u])), jnp.max(first_cols(sums[1][u][top:])))
            bound = b if bound is None else jnp.maximum(bound, b)
        return bound, acc, acc_top

    def walk(s, nsteps, run, acc):
        from_diag = isinstance(s, int) and s == 0
        steps = range(nsteps)
        offsets = range(1 - nsteps, SB_QTILES)
        members = {d: [(u, st) for st in steps for u in range(SB_QTILES) if u - st == d]
                   for d in offsets}
        order = [(u, st) for st in steps for u in range(SB_QTILES)]
        is_diag = lambda st: from_diag and st == 0
        y, values = {}, {}
        for d in offsets:
            j = base + d - s
            kbd, vbd = kv(jnp.maximum(j, 0))
            if not (from_diag and d >= 0):
                vbd = jnp.where(j >= 0, vbd, jnp.zeros_like(vbd))
            values[d] = vbd
            stacked = scores(jnp.concatenate([q_tile(u) for u, _ in members[d]], axis=0), kbd)
            for n, key in enumerate(members[d]):
                y[key] = stacked[n * t:(n + 1) * t]
        parts = {key: logs(y[key], causal if is_diag(key[1]) else None) for key in order}
        sums = []
        for st in steps:
            sums.extend(suffix_sums([parts[key][1] for key in order if key[1] == st]))
        run, acc, w = list(run), list(acc), {}
        for key, (later, rowsum) in zip(order, sums):
            u = key[0]
            if is_diag(key[1]):
                w[key] = jnp.where(causal, jnp.exp2(parts[key][0] + later), 0.0).astype(BF16)
                run[u] = rowsum
            else:
                w[key] = jnp.exp2(parts[key][0] + later + run[u]).astype(BF16)
                run[u] = run[u] + rowsum
        for d in offsets:
            ctx = jnp.dot(jnp.concatenate([w[key] for key in members[d]], axis=0), values[d],
                          preferred_element_type=F32)
            for n, (u, _) in enumerate(members[d]):
                part = ctx[n * t:(n + 1) * t]
                acc[u] = part if acc[u] is None else acc[u] + part
        top = run[0]
        for u in range(1, SB_QTILES):
            top = jnp.maximum(top, run[u])
        return jnp.max(top), run, acc

    def general():
        top, run, acc = walk(0, SB_FIRST_STEPS, [None] * SB_QTILES, [None] * SB_QTILES)

        def live(carry):
            return jnp.logical_and(carry[1] > SB_UNDERFLOW, carry[0] <= base + SB_QTILES - 1)

        def body(carry):
            s = carry[0]
            top, run, acc = walk(s, SB_STEPS_PER_CHECK, carry[2:2 + SB_QTILES],
                                 carry[2 + SB_QTILES:])
            return (s + SB_STEPS_PER_CHECK, top, *run, *acc)

        out = lax.while_loop(live, body, (jnp.int32(SB_FIRST_STEPS), top, *run, *acc))
        for u in range(SB_QTILES):
            o_ref[0, u * t:(u + 1) * t, :] = out[2 + SB_QTILES + u].astype(BF16)

    bound, acc, acc_top = None, {}, {}
    for first in range(0, SB_QTILES, SB_GROUP):
        b, a, a_top = fast(range(first, first + SB_GROUP))
        bound = b if bound is None else jnp.maximum(bound, b)
        acc.update(a)
        acc_top.update(a_top)
    finished = bound < SB_UNDERFLOW

    @pl.when(finished)
    def _():
        for u in range(SB_QTILES):
            lo_rows = u * t + SB_TOP_ROWS
            o_ref[0, u * t:lo_rows, :] = (acc[u][:SB_TOP_ROWS] + acc_top[u]).astype(BF16)
            o_ref[0, lo_rows:(u + 1) * t, :] = acc[u][SB_TOP_ROWS:].astype(BF16)

    @pl.when(jnp.logical_not(finished))
    def _():
        general()


def _sb_attention(q, k, v, cum, cum3):
    B, S, _ = q.shape
    rows = SB_TILE * SB_QTILES
    pairs = SB_WIDTH // LANES
    qspec = pl.BlockSpec((1, rows, LANES), lambda b, p, i: (b, i, p))
    kvspec = pl.BlockSpec((1, S, 2 * LANES), lambda b, p, i: (b, 0, p))
    return pl.pallas_call(
        _sb_kernel,
        grid=(B, pairs, S // rows),
        in_specs=[qspec, kvspec, kvspec, _resident(cum.shape), _resident(cum3.shape)],
        out_specs=qspec,
        out_shape=jax.ShapeDtypeStruct((B, S, SB_WIDTH), BF16),
        compiler_params=_params("parallel", "parallel", "arbitrary"),
        name="sb_attn",
    )(q, k, v, cum, cum3)


def _ret_kernel(q_ref, k_ref, v_ref, g_ref, decay_ref, qd_ref, kd_ref, cd_ref, o_ref, state_ref):
    @pl.when(pl.program_id(2) == 0)
    def _():
        state_ref[...] = jnp.zeros_like(state_ref)

    c = RET_CHUNK
    chunks = [slice(ci * c, (ci + 1) * c) for ci in range(RET_ROWS // c)]
    inner, chunk_kv = [], []
    for sl in chunks:
        q = q_ref[0, sl, :]
        k = k_ref[0, sl, :]
        v = v_ref[0, sl, :]
        s = lax.dot_general(q, k, (((1,), (1,)), ((), ())), preferred_element_type=F32) * decay_ref[0]
        inner.append(jnp.dot(s.astype(BF16), v, preferred_element_type=F32))
        kdec = (k.astype(F32) * kd_ref[0]).astype(BF16)
        chunk_kv.append(lax.dot_general(kdec, v, (((0,), (0,)), ((), ())),
                                        preferred_element_type=F32))
    st = state_ref[...]
    before = []
    for kv in chunk_kv:
        before.append(st.astype(BF16))
        st = st * cd_ref[0] + kv
    state_ref[...] = st
    for sl, part, state in zip(chunks, inner, before):
        o = part + jnp.dot(q_ref[0, sl, :], state, preferred_element_type=F32) * qd_ref[0]
        mu = jnp.mean(o, axis=-1, keepdims=True)
        oc = o - mu
        var = jnp.mean(oc * oc, axis=-1, keepdims=True)
        normed = oc * lax.rsqrt(var + LN_EPS)
        o_ref[0, sl, :] = normed.astype(BF16) * g_ref[0, sl, :]


def _retention(rq, rk, rv, rg, tables):
    B, S, _ = rq.shape
    c = RET_CHUNK
    qk = pl.BlockSpec((1, RET_ROWS, RET_QK_DIM), lambda b, h, r: (b, r, h))
    vv = pl.BlockSpec((1, RET_ROWS, RET_V_DIM), lambda b, h, r: (b, r, h))
    head = lambda rows, cols: pl.BlockSpec((1, rows, cols), lambda b, h, r: (h, 0, 0))
    return pl.pallas_call(
        _ret_kernel,
        grid=(B, RET_HEADS, S // RET_ROWS),
        in_specs=[qk, qk, vv, vv, head(c, c), head(c, RET_V_DIM), head(c, RET_QK_DIM),
                  head(RET_QK_DIM, RET_V_DIM)],
        out_specs=vv,
        out_shape=jax.ShapeDtypeStruct((B, S, RET_V_WIDTH), BF16),
        scratch_shapes=[pltpu.VMEM((RET_QK_DIM, RET_V_DIM), F32)],
        compiler_params=_params("parallel", "parallel", "arbitrary"),
        name="retention",
    )(rq, rk, rv, rg, *tables)


def _retention_tables():
    c = RET_CHUNK
    log_gamma = np.log1p(-np.exp2(-5.0 - np.arange(RET_HEADS, dtype=np.float64)))
    idx = np.arange(c, dtype=np.float64)
    rel = idx[:, None] - idx[None, :]
    lg = log_gamma[:, None, None]
    decay = np.where(rel[None] >= 0, np.exp(lg * np.maximum(rel, 0.0)[None]), 0.0)
    qd = np.broadcast_to(np.exp(lg * (idx + 1.0)[None, :, None]), (RET_HEADS, c, RET_V_DIM))
    kd = np.broadcast_to(np.exp(lg * (c - 1.0 - idx)[None, :, None]), (RET_HEADS, c, RET_QK_DIM))
    cd = np.broadcast_to(np.exp(lg * float(c)), (RET_HEADS, RET_QK_DIM, RET_V_DIM))
    return tuple(jnp.asarray(t, F32) for t in (decay, qd, kd, cd))


def _mix_kernel(a_ref, r_ref, gate_ref, x_ref, wsb_ref, wret_ref, wmix_ref, g_ref, b_ref, o_ref):
    for sub in range(DENSE_SUBTILES):
        rows = slice(sub * SUB_ROWS, (sub + 1) * SUB_ROWS)
        y_sb = jnp.dot(a_ref[rows, :], wsb_ref[...], preferred_element_type=F32)
        y_ret = jnp.dot(r_ref[rows, :], wret_ref[...], preferred_element_type=F32)
        merged = (gate_ref[rows, :D_MODEL] * y_sb.astype(BF16)
                  + gate_ref[rows, D_MODEL:] * y_ret.astype(BF16))
        mix = jnp.dot(merged, wmix_ref[...], preferred_element_type=F32)
        o_ref[rows, :] = _residual_norm(x_ref[rows, :], mix, g_ref[...], b_ref[...])


def _mix_ln1(attn, retg, gates, x2d, w_sb_o, w_ret_o, w_mix_o, g, b):
    T = x2d.shape[0]
    tm = SUB_ROWS * DENSE_SUBTILES
    row = lambda w: pl.BlockSpec((tm, w), lambda i: (i, 0))
    return pl.pallas_call(
        _mix_kernel,
        grid=(T // tm,),
        in_specs=[row(SB_WIDTH), row(RET_V_WIDTH), row(N_BRANCHES * D_MODEL), row(D_MODEL),
                  _resident(w_sb_o.shape), _resident(w_ret_o.shape), _resident(w_mix_o.shape),
                  _resident(g.shape), _resident(b.shape)],
        out_specs=row(D_MODEL),
        out_shape=jax.ShapeDtypeStruct((T, D_MODEL), F32),
        compiler_params=_params("parallel"),
        name="mix_ln1",
    )(attn, retg, gates, x2d, w_sb_o, w_ret_o, w_mix_o, g, b)


def _xattn_kernel(x_ref, m_ref, wkv_ref, wq_ref, wo_ref, g_ref, b_ref, o_ref, k_ref, v_ref, ctx_ref):
    @pl.when(pl.program_id(1) == 0)
    def _():
        mb = m_ref[0].astype(BF16)
        k_ref[...] = jnp.dot(mb, wkv_ref[:, :D_MODEL], preferred_element_type=F32).astype(BF16)
        v_ref[...] = jnp.dot(mb, wkv_ref[:, D_MODEL:], preferred_element_type=F32).astype(BF16)

    for sub in range(DENSE_SUBTILES):
        rows = slice(sub * SUB_ROWS, (sub + 1) * SUB_ROWS)
        x = x_ref[0, rows, :]
        q = jnp.dot(x.astype(BF16), wq_ref[...], preferred_element_type=F32).astype(BF16)
        for h in range(MEM_HEADS):
            sl = slice(h * MEM_HEAD_DIM, (h + 1) * MEM_HEAD_DIM)
            s = lax.dot_general(q[:, sl], k_ref[:, sl], (((1,), (1,)), ((), ())),
                                preferred_element_type=F32)
            p = jnp.exp2(s - jnp.max(s, axis=-1, keepdims=True))
            denom = jnp.sum(p, axis=-1, keepdims=True)
            ctx = jnp.dot(p.astype(BF16), v_ref[:, sl], preferred_element_type=F32)
            ctx_ref[rows, sl] = (ctx / denom).astype(BF16)
        xa = jnp.dot(ctx_ref[rows, :], wo_ref[...], preferred_element_type=F32)
        o_ref[0, rows, :] = _residual_norm(x, xa, g_ref[...], b_ref[...])


def _xattn_ln2(x3d, mem, w_kv, w_q, w_o, g, b):
    B, S, _ = x3d.shape
    M = mem.shape[1]
    tm = SUB_ROWS * DENSE_SUBTILES
    row = pl.BlockSpec((1, tm, D_MODEL), lambda bi, i: (bi, i, 0))
    mem_blk = pl.BlockSpec((1, M, D_MODEL), lambda bi, i: (bi, 0, 0))
    return pl.pallas_call(
        _xattn_kernel,
        grid=(B, S // tm),
        in_specs=[row, mem_blk, _resident(w_kv.shape), _resident(w_q.shape), _resident(w_o.shape),
                  _resident(g.shape), _resident(b.shape)],
        out_specs=row,
        out_shape=jax.ShapeDtypeStruct((B, S, D_MODEL), F32),
        scratch_shapes=[pltpu.VMEM((M, D_MODEL), BF16), pltpu.VMEM((M, D_MODEL), BF16),
                        pltpu.VMEM((tm, D_MODEL), BF16)],
        compiler_params=_params("parallel", "arbitrary"),
        name="xattn_ln2",
    )(x3d, mem, w_kv, w_q, w_o, g, b)


def _ffn_kernel(x_ref, win_ref, wout_ref, g_ref, b_ref, o_ref, hid_ref):
    for sub in range(FFN_ROWS // FFN_SUB_ROWS):
        rows = slice(sub * FFN_SUB_ROWS, (sub + 1) * FFN_SUB_ROWS)
        x = x_ref[rows, :]
        xb = x.astype(BF16)
        for c in range(FFN_HIDDEN // FFN_CHUNK):
            lo = c * FFN_CHUNK
            a = jnp.dot(xb, win_ref[:, lo:lo + FFN_CHUNK], preferred_element_type=F32)
            gte = jnp.dot(xb, win_ref[:, FFN_HIDDEN + lo:FFN_HIDDEN + lo + FFN_CHUNK],
                          preferred_element_type=F32)
            hid_ref[rows, lo:lo + FFN_CHUNK] = (a * _sigmoid(a) * gte).astype(BF16)
        ff = jnp.dot(hid_ref[rows, :], wout_ref[...], preferred_element_type=F32)
        o_ref[rows, :] = _residual_norm(x, ff, g_ref[...], b_ref[...])


def _ffn_ln3(x2d, w_in, w_out, g, b):
    T = x2d.shape[0]
    tm = FFN_ROWS
    row = pl.BlockSpec((tm, D_MODEL), lambda i: (i, 0))
    return pl.pallas_call(
        _ffn_kernel,
        grid=(T // tm,),
        in_specs=[row, _resident(w_in.shape), _resident(w_out.shape),
                  _resident(g.shape), _resident(b.shape)],
        out_specs=row,
        out_shape=jax.ShapeDtypeStruct((T, D_MODEL), F32),
        scratch_shapes=[pltpu.VMEM((tm, FFN_HIDDEN), BF16)],
        compiler_params=_params("parallel"),
        name="ffn_ln3",
    )(x2d, w_in, w_out, g, b)


def _rope_tables(seq):
    half = RET_QK_DIM // 2
    inv = 1.0 / (ROPE_BASE ** (np.arange(half, dtype=np.float64) / half))
    ang = np.arange(seq, dtype=np.float64)[:, None] * inv[None, :]
    cos, sin = np.cos(ang), np.sin(ang)
    return (jnp.asarray(np.concatenate([cos, cos], axis=-1), F32),
            jnp.asarray(np.concatenate([-sin, sin], axis=-1), F32))


def _cumsum_matrices():
    t = SB_TILE
    r = np.arange(t)
    upper = (r[:, None] > r[None, :]).astype(np.float32)
    ones = np.ones((t, t), np.float32)
    zero = np.zeros((t, t), np.float32)
    half = np.concatenate([upper, ones], axis=1)
    cum = np.concatenate([half, half], axis=0)
    both = lambda m: np.concatenate([np.concatenate([m, zero], axis=1),
                                     np.concatenate([zero, m], axis=1)], axis=0)
    cum3 = np.concatenate([both(upper)] + [both(ones)] * 2, axis=0)
    return jnp.asarray(cum, BF16), jnp.asarray(cum3, BF16)


def kernel(x, mem, w_in, b_gate, w_sb_o, w_ret_o, w_mix_o, ln1_g, ln1_b, w_mem_q, w_mem_kv,
           w_mem_o, ln2_g, ln2_b, w_ffn_in, w_ffn_out, ln3_g, ln3_b):
    B, S, D = x.shape
    assert D == D_MODEL and w_in.shape == (DEPTH, D_MODEL, IN_WIDTH)
    assert S % RET_ROWS == 0 and (B * S) % ROW_TILE == 0 and S % ROW_TILE == 0
    cos_t, sin_t = _rope_tables(S)
    cum, cum3 = _cumsum_matrices()
    ret_tables = _retention_tables()
    x2d = x.reshape(B * S, D)
    for l in range(DEPTH):
        bf = lambda w: w[l].astype(BF16)
        bf_over_alpha = lambda w: (w[l] * (1.0 / DN_ALPHA)).astype(BF16)
        vec = lambda p: p[l][None, :]
        sbq, sbk, sbv, rq, rk, rv, rg, gates = _inproj(
            x2d, bf(w_in), vec(b_gate), cos_t, sin_t, S)
        seq3 = lambda a: a.reshape(B, S, a.shape[-1])
        attn = _sb_attention(seq3(sbq), seq3(sbk), seq3(sbv), cum, cum3)
        retg = _retention(seq3(rq), seq3(rk), seq3(rv), seq3(rg), ret_tables)
        x1 = _mix_ln1(attn.reshape(B * S, SB_WIDTH), retg.reshape(B * S, RET_V_WIDTH), gates, x2d,
                      bf(w_sb_o), bf(w_ret_o), bf_over_alpha(w_mix_o), vec(ln1_g), vec(ln1_b))
        w_q_scaled = (w_mem_q[l] * (MEM_HEAD_DIM ** -0.5 * LOG2_E)).astype(BF16)
        x2 = _xattn_ln2(x1.reshape(B, S, D), mem, bf(w_mem_kv), w_q_scaled, bf_over_alpha(w_mem_o),
                        vec(ln2_g), vec(ln2_b))
        x2d = _ffn_ln3(x2.reshape(B * S, D), bf(w_ffn_in), bf_over_alpha(w_ffn_out), vec(ln3_g), vec(ln3_b))
    return x2d.reshape(B, S, D)
```

```python
import functools

import jax
import jax.numpy as jnp
import numpy as np
from jax import lax
from jax.experimental import pallas as pl
from jax.experimental.pallas import tpu as pltpu

F32 = jnp.float32
BF16 = jnp.bfloat16

D_MODEL = 1024
DEPTH = 1
SB_HEADS = 8
SB_HEAD_DIM = 64
SB_WIDTH = SB_HEADS * SB_HEAD_DIM
RET_HEADS = 4
RET_QK_DIM = 128
RET_V_DIM = 256
RET_QK_WIDTH = RET_HEADS * RET_QK_DIM
RET_V_WIDTH = RET_HEADS * RET_V_DIM
ROPE_BASE = 10000.0
N_BRANCHES = 2
OFF_SB_Q = 0
OFF_SB_K = OFF_SB_Q + SB_WIDTH
OFF_SB_V = OFF_SB_K + SB_WIDTH
OFF_RET_Q = OFF_SB_V + SB_WIDTH
OFF_RET_K = OFF_RET_Q + RET_QK_WIDTH
OFF_RET_V = OFF_RET_K + RET_QK_WIDTH
OFF_RET_G = OFF_RET_V + RET_V_WIDTH
OFF_GATE = OFF_RET_G + RET_V_WIDTH
IN_WIDTH = OFF_GATE + N_BRANCHES * D_MODEL
MEM_HEADS = 4
MEM_HEAD_DIM = D_MODEL // MEM_HEADS
FFN_HIDDEN = 2816
DN_ALPHA = (2.0 * DEPTH) ** 0.25
LN_EPS = 1e-5

LANES = 128
VMEM_LIMIT_BYTES = 56 * 1024 * 1024
ROW_TILE = 512
SUB_ROWS = 256
DENSE_SUBTILES = 4
XATTN_SUBTILES = 4
SB_TILE = 128
SB_QTILES = 16
SB_GROUP = 8
SB_FAST_PHASES = 5
SB_TOP_ROWS = 32
SB_FIRST_STEPS = 3
SB_STEPS_PER_CHECK = 2
SB_UNDERFLOW = -150.5
LOG2_E = 1.4426950408889634
RET_CHUNK = 256
RET_ROWS = 4096
FFN_CHUNK = 256
FFN_ROWS = 1024
FFN_SUB_ROWS = 256


def _resident(shape):
    zeros = (0,) * len(shape)
    return pl.BlockSpec(shape, lambda *_: zeros, pipeline_mode=pl.Buffered(1))


def _params(*sem):
    return pltpu.CompilerParams(dimension_semantics=sem, vmem_limit_bytes=VMEM_LIMIT_BYTES)


def _sigmoid(x):
    return 0.5 * jnp.tanh(0.5 * x) + 0.5


def _residual_norm(x, branch_over_alpha, g, b):
    y = x + branch_over_alpha
    mu = jnp.mean(y, axis=-1, keepdims=True)
    yc = y - mu
    var = jnp.mean(yc * yc, axis=-1, keepdims=True)
    return yc * lax.rsqrt(var + LN_EPS / (DN_ALPHA * DN_ALPHA)) * g + b


def _inproj_kernel(x_ref, w_ref, bg_ref, cos_ref, sin_ref,
                   sbq_ref, sbk_ref, sbv_ref, rq_ref, rk_ref, rv_ref, rg_ref, gate_ref):
    xb = x_ref[...].astype(BF16)

    def proj(off, width):
        return jnp.dot(xb, w_ref[:, off:off + width], preferred_element_type=F32)

    for br in range(N_BRANCHES):
        sl = slice(br * D_MODEL, (br + 1) * D_MODEL)
        hg = proj(OFF_GATE + br * D_MODEL, D_MODEL) + bg_ref[:, sl]
        gate_ref[:, sl] = _sigmoid(hg).astype(BF16)
    g = proj(OFF_RET_G, RET_V_WIDTH)
    rg_ref[...] = (g * _sigmoid(g)).astype(BF16)

    def store_head_split(h, out_ref):
        hb = h.astype(BF16)
        first = lax.broadcasted_iota(jnp.int32, (hb.shape[0], LANES), 1) < SB_HEAD_DIM
        zero = jnp.zeros((hb.shape[0], LANES), BF16)
        for p in range(SB_WIDTH // LANES):
            pair = hb[:, p * LANES:(p + 1) * LANES]
            out_ref[:, 2 * p * LANES:(2 * p + 1) * LANES] = jnp.where(first, pair, zero)
            out_ref[:, (2 * p + 1) * LANES:(2 * p + 2) * LANES] = jnp.where(first, zero, pair)

    store_head_split(proj(OFF_SB_K, SB_WIDTH), sbk_ref)
    store_head_split(proj(OFF_SB_V, SB_WIDTH), sbv_ref)

    cos = cos_ref[...]
    sin = sin_ref[...]

    def rope_store(h, out_ref, scale):
        for hd in range(RET_HEADS):
            sl = slice(hd * RET_QK_DIM, (hd + 1) * RET_QK_DIM)
            xh = h[:, sl]
            rot = xh * cos + pltpu.roll(xh, RET_QK_DIM // 2, 1) * sin
            if scale != 1.0:
                rot = rot * scale
            out_ref[:, sl] = rot.astype(BF16)

    rope_store(proj(OFF_RET_Q, RET_QK_WIDTH), rq_ref, RET_QK_DIM ** -0.5)
    rope_store(proj(OFF_RET_K, RET_QK_WIDTH), rk_ref, 1.0)
    sbq_ref[...] = (proj(OFF_SB_Q, SB_WIDTH) * (SB_HEAD_DIM ** -0.5 * LOG2_E)).astype(BF16)
    rv_ref[...] = proj(OFF_RET_V, RET_V_WIDTH).astype(BF16)


def _inproj(x2d, w_in, b_gate, cos_t, sin_t, seq):
    T = x2d.shape[0]
    tm = ROW_TILE
    tiles_per_seq = seq // tm
    row = lambda w: pl.BlockSpec((tm, w), lambda i: (i, 0))
    tab = pl.BlockSpec((tm, LANES), lambda i: (i % tiles_per_seq, 0))
    widths = (SB_WIDTH, 2 * SB_WIDTH, 2 * SB_WIDTH, RET_QK_WIDTH, RET_QK_WIDTH,
              RET_V_WIDTH, RET_V_WIDTH, N_BRANCHES * D_MODEL)
    return pl.pallas_call(
        _inproj_kernel,
        grid=(T // tm,),
        in_specs=[row(D_MODEL), _resident((D_MODEL, IN_WIDTH)),
                  _resident((1, N_BRANCHES * D_MODEL)), tab, tab],
        out_specs=[row(w) for w in widths],
        out_shape=[jax.ShapeDtypeStruct((T, w), BF16) for w in widths],
        compiler_params=_params("parallel"),
        name="inproj",
    )(x2d, w_in, b_gate, cos_t, sin_t)


def _sb_kernel(q_ref, k_ref, v_ref, cum_ref, cum3_ref, o_ref):
    blk = pl.program_id(2)
    t = SB_TILE
    cum = cum_ref[...]
    row = lax.broadcasted_iota(jnp.int32, (t, 2 * t), 0)
    col = lax.broadcasted_iota(jnp.int32, (t, 2 * t), 1)
    causal = jnp.where(col >= t, col - t, col) < row

    def stack(x):
        return jnp.concatenate([x[:, :LANES], x[:, LANES:]], axis=0)

    def kv(j):
        start = pl.multiple_of(j * t, t)
        return stack(k_ref[0, pl.ds(start, t), :]), stack(v_ref[0, pl.ds(start, t), :])

    def q_tile(u):
        return q_ref[0, u * t:(u + 1) * t, :]

    def scores(q_rows, kbd):
        return lax.dot_general(q_rows, kbd, (((1,), (1,)), ((), ())), preferred_element_type=F32)

    def logs(y, mask):
        log_beta = jnp.minimum(y, 0.0) - jnp.log2(1.0 + jnp.exp2(-jnp.abs(y)))
        log_rem = log_beta - y
        if mask is not None:
            log_rem = jnp.where(mask, log_rem, 0.0)
        hi = log_rem.astype(BF16)
        lo = (log_rem - hi.astype(F32)).astype(BF16)
        rows = [jnp.concatenate([hi[:, h * t:(h + 1) * t], lo[:, h * t:(h + 1) * t]], axis=1)
                for h in range(2)]
        return log_beta, jnp.concatenate(rows, axis=0)

    def suffix_sums(operands):
        sums = jnp.dot(jnp.concatenate(operands, axis=0), cum, preferred_element_type=F32)
        out = []
        for n in range(len(operands)):
            a = sums[2 * n * t:(2 * n + 1) * t]
            b = sums[(2 * n + 1) * t:(2 * n + 2) * t]
            out.append((jnp.concatenate([a[:, :t], b[:, :t]], axis=1),
                        jnp.concatenate([a[:, t:], b[:, t:]], axis=1)))
        return out

    base = blk * SB_QTILES

    def fast_logs(y, mask):
        log_beta = jnp.minimum(y, 0.0) - jnp.log2(1.0 + jnp.exp2(-jnp.abs(y)))
        log_rem = log_beta - y
        if mask is not None:
            log_rem = jnp.where(mask, log_rem, 0.0)
        return log_beta, log_rem.astype(BF16)

    def fast(chains):
        top = SB_TOP_ROWS
        nrows = (t, t, top)
        offsets = range(chains[0] + 1 - len(nrows), chains[-1] + 1)
        members = {d: [(d + st, st) for st in range(len(nrows)) if d + st in chains]
                   for d in offsets}
        y, values = {}, {}
        for d in offsets:
            j = base + d
            kbd, vbd = kv(jnp.maximum(j, 0))
            if d < 0:
                vbd = jnp.where(j >= 0, vbd, jnp.zeros_like(vbd))
            values[d] = vbd
            q_rows = [q_ref[0, u * t:u * t + nrows[st], :] for u, st in members[d]]
            stacked = scores(jnp.concatenate(q_rows, axis=0), kbd)
            off = 0
            for u, st in members[d]:
                y[(u, st)] = stacked[off:off + nrows[st]]
                off += nrows[st]
        yield
        lb, hi = {}, {}
        for st in range(len(nrows)):
            for u in chains:
                lb[(u, st)], hi[(u, st)] = fast_logs(y[(u, st)], causal if st == 0 else None)
        yield
        cum3 = cum3_ref[...]

        def later(st):
            n = nrows[st]
            ops = [jnp.concatenate([hi[(u, st - i)][:n] for i in range(st + 1)], axis=1)
                   for u in chains]
            out = jnp.dot(jnp.concatenate(ops, axis=0), cum3[:2 * t * (st + 1)],
                          preferred_element_type=F32)
            return {u: out[i * n:(i + 1) * n] for i, u in enumerate(chains)}

        sums = [later(st) for st in range(len(nrows))]
        yield
        w = {}
        for st in range(len(nrows)):
            for u in chains:
                e = jnp.exp2(lb[(u, st)] + sums[st][u])
                w[(u, st)] = (jnp.where(causal, e, 0.0) if st == 0 else e).astype(BF16)
        yield
        acc, acc_top = {u: None for u in chains}, {}
        for d in offsets:
            ctx = jnp.dot(jnp.concatenate([w[key] for key in members[d]], axis=0), values[d],
                          preferred_element_type=F32)
            off = 0
            for u, st in members[d]:
                part = ctx[off:off + nrows[st]]
                off += nrows[st]
                if st == len(nrows) - 1:
                    acc_top[u] = part
                else:
                    acc[u] = part if acc[u] is None else acc[u] + part
        first_cols = lambda a: jnp.maximum(a[:, 0:1], a[:, t:t + 1])
        bound = None
        for u in chains:
            b = jnp.maximum(jnp.max(first_cols(sums[2][u])), jnp.max(first_cols(sums[1][u][top:])))
            bound = b if bound is None else jnp.maximum(bound, b)
        yield bound, acc, acc_top

    def walk(s, nsteps, run, acc):
        from_diag = isinstance(s, int) and s == 0
        steps = range(nsteps)
        offsets = range(1 - nsteps, SB_QTILES)
        members = {d: [(u, st) for st in steps for u in range(SB_QTILES) if u - st == d]
                   for d in offsets}
        order = [(u, st) for st in steps for u in range(SB_QTILES)]
        is_diag = lambda st: from_diag and st == 0
        y, values = {}, {}
        for d in offsets:
            j = base + d - s
            kbd, vbd = kv(jnp.maximum(j, 0))
            if not (from_diag and d >= 0):
                vbd = jnp.where(j >= 0, vbd, jnp.zeros_like(vbd))
            values[d] = vbd
            stacked = scores(jnp.concatenate([q_tile(u) for u, _ in members[d]], axis=0), kbd)
            for n, key in enumerate(members[d]):
                y[key] = stacked[n * t:(n + 1) * t]
        parts = {key: logs(y[key], causal if is_diag(key[1]) else None) for key in order}
        sums = []
        for st in steps:
            sums.extend(suffix_sums([parts[key][1] for key in order if key[1] == st]))
        run, acc, w = list(run), list(acc), {}
        for key, (later, rowsum) in zip(order, sums):
            u = key[0]
            if is_diag(key[1]):
                w[key] = jnp.where(causal, jnp.exp2(parts[key][0] + later), 0.0).astype(BF16)
                run[u] = rowsum
            else:
                w[key] = jnp.exp2(parts[key][0] + later + run[u]).astype(BF16)
                run[u] = run[u] + rowsum
        for d in offsets:
            ctx = jnp.dot(jnp.concatenate([w[key] for key in members[d]], axis=0), values[d],
                          preferred_element_type=F32)
            for n, (u, _) in enumerate(members[d]):
                part = ctx[n * t:(n + 1) * t]
                acc[u] = part if acc[u] is None else acc[u] + part
        top = run[0]
        for u in range(1, SB_QTILES):
            top = jnp.maximum(top, run[u])
        return jnp.max(top), run, acc

    def general():
        top, run, acc = walk(0, SB_FIRST_STEPS, [None] * SB_QTILES, [None] * SB_QTILES)

        def live(carry):
            return jnp.logical_and(carry[1] > SB_UNDERFLOW, carry[0] <= base + SB_QTILES - 1)

        def body(carry):
            s = carry[0]
            top, run, acc = walk(s, SB_STEPS_PER_CHECK, carry[2:2 + SB_QTILES],
                                 carry[2 + SB_QTILES:])
            return (s + SB_STEPS_PER_CHECK, top, *run, *acc)

        out = lax.while_loop(live, body, (jnp.int32(SB_FIRST_STEPS), top, *run, *acc))
        for u in range(SB_QTILES):
            o_ref[0, u * t:(u + 1) * t, :] = out[2 + SB_QTILES + u].astype(BF16)

    groups = [fast(range(first, first + SB_GROUP)) for first in range(0, SB_QTILES, SB_GROUP)]
    results = [None] * len(groups)
    for slot in range(SB_FAST_PHASES + len(groups) - 1):
        for g, group in enumerate(groups):
            if 0 <= slot - g < SB_FAST_PHASES:
                results[g] = next(group)
    bound, acc, acc_top = None, {}, {}
    for b, a, a_top in results:
        bound = b if bound is None else jnp.maximum(bound, b)
        acc.update(a)
        acc_top.update(a_top)
    finished = bound < SB_UNDERFLOW

    @pl.when(finished)
    def _():
        for u in range(SB_QTILES):
            lo_rows = u * t + SB_TOP_ROWS
            o_ref[0, u * t:lo_rows, :] = (acc[u][:SB_TOP_ROWS] + acc_top[u]).astype(BF16)
            o_ref[0, lo_rows:(u + 1) * t, :] = acc[u][SB_TOP_ROWS:].astype(BF16)

    @pl.when(jnp.logical_not(finished))
    def _():
        general()


def _sb_attention(q, k, v, cum, cum3):
    B, S, _ = q.shape
    rows = SB_TILE * SB_QTILES
    pairs = SB_WIDTH // LANES
    qspec = pl.BlockSpec((1, rows, LANES), lambda b, p, i: (b, i, p))
    kvspec = pl.BlockSpec((1, S, 2 * LANES), lambda b, p, i: (b, 0, p))
    return pl.pallas_call(
        _sb_kernel,
        grid=(B, pairs, S // rows),
        in_specs=[qspec, kvspec, kvspec, _resident(cum.shape), _resident(cum3.shape)],
        out_specs=qspec,
        out_shape=jax.ShapeDtypeStruct((B, S, SB_WIDTH), BF16),
        compiler_params=_params("parallel", "parallel", "arbitrary"),
        name="sb_attn",
    )(q, k, v, cum, cum3)


def _ret_kernel(q_ref, k_ref, v_ref, g_ref, decay_ref, qd_ref, kd_ref, cd_ref, o_ref, state_ref):
    @pl.when(pl.program_id(2) == 0)
    def _():
        state_ref[...] = jnp.zeros_like(state_ref)

    c = RET_CHUNK
    chunks = [slice(ci * c, (ci + 1) * c) for ci in range(RET_ROWS // c)]
    inner, chunk_kv = [], []
    for sl in chunks:
        q = q_ref[0, sl, :]
        k = k_ref[0, sl, :]
        v = v_ref[0, sl, :]
        s = lax.dot_general(q, k, (((1,), (1,)), ((), ())), preferred_element_type=F32) * decay_ref[0]
        inner.append(jnp.dot(s.astype(BF16), v, preferred_element_type=F32))
        kdec = (k.astype(F32) * kd_ref[0]).astype(BF16)
        chunk_kv.append(lax.dot_general(kdec, v, (((0,), (0,)), ((), ())),
                                        preferred_element_type=F32))
    st = state_ref[...]
    before = []
    for kv in chunk_kv:
        before.append(st.astype(BF16))
        st = st * cd_ref[0] + kv
    state_ref[...] = st
    for sl, part, state in zip(chunks, inner, before):
        o = part + jnp.dot(q_ref[0, sl, :], state, preferred_element_type=F32) * qd_ref[0]
        mu = jnp.mean(o, axis=-1, keepdims=True)
        oc = o - mu
        var = jnp.mean(oc * oc, axis=-1, keepdims=True)
        normed = oc * lax.rsqrt(var + LN_EPS)
        o_ref[0, sl, :] = normed.astype(BF16) * g_ref[0, sl, :]


def _retention(rq, rk, rv, rg, tables):
    B, S, _ = rq.shape
    c = RET_CHUNK
    qk = pl.BlockSpec((1, RET_ROWS, RET_QK_DIM), lambda b, h, r: (b, r, h))
    vv = pl.BlockSpec((1, RET_ROWS, RET_V_DIM), lambda b, h, r: (b, r, h))
    head = lambda rows, cols: pl.BlockSpec((1, rows, cols), lambda b, h, r: (h, 0, 0))
    return pl.pallas_call(
        _ret_kernel,
        grid=(B, RET_HEADS, S // RET_ROWS),
        in_specs=[qk, qk, vv, vv, head(c, c), head(c, RET_V_DIM), head(c, RET_QK_DIM),
                  head(RET_QK_DIM, RET_V_DIM)],
        out_specs=vv,
        out_shape=jax.ShapeDtypeStruct((B, S, RET_V_WIDTH), BF16),
        scratch_shapes=[pltpu.VMEM((RET_QK_DIM, RET_V_DIM), F32)],
        compiler_params=_params("parallel", "parallel", "arbitrary"),
        name="retention",
    )(rq, rk, rv, rg, *tables)


def _retention_tables():
    c = RET_CHUNK
    log_gamma = np.log1p(-np.exp2(-5.0 - np.arange(RET_HEADS, dtype=np.float64)))
    idx = np.arange(c, dtype=np.float64)
    rel = idx[:, None] - idx[None, :]
    lg = log_gamma[:, None, None]
    decay = np.where(rel[None] >= 0, np.exp(lg * np.maximum(rel, 0.0)[None]), 0.0)
    qd = np.broadcast_to(np.exp(lg * (idx + 1.0)[None, :, None]), (RET_HEADS, c, RET_V_DIM))
    kd = np.broadcast_to(np.exp(lg * (c - 1.0 - idx)[None, :, None]), (RET_HEADS, c, RET_QK_DIM))
    cd = np.broadcast_to(np.exp(lg * float(c)), (RET_HEADS, RET_QK_DIM, RET_V_DIM))
    return tuple(jnp.asarray(t, F32) for t in (decay, qd, kd, cd))


def _mix_kernel(a_ref, r_ref, gate_ref, x_ref, wsb_ref, wret_ref, wmix_ref, g_ref, b_ref, o_ref):
    for sub in range(DENSE_SUBTILES):
        rows = slice(sub * SUB_ROWS, (sub + 1) * SUB_ROWS)
        y_sb = jnp.dot(a_ref[rows, :], wsb_ref[...], preferred_element_type=F32)
        y_ret = jnp.dot(r_ref[rows, :], wret_ref[...], preferred_element_type=F32)
        merged = (gate_ref[rows, :D_MODEL] * y_sb.astype(BF16)
                  + gate_ref[rows, D_MODEL:] * y_ret.astype(BF16))
        mix = jnp.dot(merged, wmix_ref[...], preferred_element_type=F32)
        o_ref[rows, :] = _residual_norm(x_ref[rows, :], mix, g_ref[...], b_ref[...])


def _mix_ln1(attn, retg, gates, x2d, w_sb_o, w_ret_o, w_mix_o, g, b):
    T = x2d.shape[0]
    tm = SUB_ROWS * DENSE_SUBTILES
    row = lambda w: pl.BlockSpec((tm, w), lambda i: (i, 0))
    return pl.pallas_call(
        _mix_kernel,
        grid=(T // tm,),
        in_specs=[row(SB_WIDTH), row(RET_V_WIDTH), row(N_BRANCHES * D_MODEL), row(D_MODEL),
                  _resident(w_sb_o.shape), _resident(w_ret_o.shape), _resident(w_mix_o.shape),
                  _resident(g.shape), _resident(b.shape)],
        out_specs=row(D_MODEL),
        out_shape=jax.ShapeDtypeStruct((T, D_MODEL), F32),
        compiler_params=_params("parallel"),
        name="mix_ln1",
    )(attn, retg, gates, x2d, w_sb_o, w_ret_o, w_mix_o, g, b)


def _xattn_kernel(x_ref, m_ref, wkv_ref, wq_ref, wo_ref, g_ref, b_ref, o_ref, k_ref, v_ref, ctx_ref):
    @pl.when(pl.program_id(1) == 0)
    def _():
        mb = m_ref[0].astype(BF16)
        k_ref[...] = jnp.dot(mb, wkv_ref[:, :D_MODEL], preferred_element_type=F32).astype(BF16)
        v_ref[...] = jnp.dot(mb, wkv_ref[:, D_MODEL:], preferred_element_type=F32).astype(BF16)

    for sub in range(XATTN_SUBTILES):
        rows = slice(sub * SUB_ROWS, (sub + 1) * SUB_ROWS)
        x = x_ref[0, rows, :]
        q = jnp.dot(x.astype(BF16), wq_ref[...], preferred_element_type=F32).astype(BF16)
        for h in range(MEM_HEADS):
            sl = slice(h * MEM_HEAD_DIM, (h + 1) * MEM_HEAD_DIM)
            s = lax.dot_general(q[:, sl], k_ref[:, sl], (((1,), (1,)), ((), ())),
                                preferred_element_type=F32)
            p = jnp.exp2(s - jnp.max(s, axis=-1, keepdims=True))
            denom = jnp.sum(p, axis=-1, keepdims=True)
            ctx = jnp.dot(p.astype(BF16), v_ref[:, sl], preferred_element_type=F32)
            ctx_ref[rows, sl] = (ctx / denom).astype(BF16)
        xa = jnp.dot(ctx_ref[rows, :], wo_ref[...], preferred_element_type=F32)
        o_ref[0, rows, :] = _residual_norm(x, xa, g_ref[...], b_ref[...])


def _xattn_ln2(x3d, mem, w_kv, w_q, w_o, g, b):
    B, S, _ = x3d.shape
    M = mem.shape[1]
    tm = SUB_ROWS * XATTN_SUBTILES
    row = pl.BlockSpec((1, tm, D_MODEL), lambda bi, i: (bi, i, 0))
    mem_blk = pl.BlockSpec((1, M, D_MODEL), lambda bi, i: (bi, 0, 0))
    return pl.pallas_call(
        _xattn_kernel,
        grid=(B, S // tm),
        in_specs=[row, mem_blk, _resident(w_kv.shape), _resident(w_q.shape), _resident(w_o.shape),
                  _resident(g.shape), _resident(b.shape)],
        out_specs=row,
        out_shape=jax.ShapeDtypeStruct((B, S, D_MODEL), F32),
        scratch_shapes=[pltpu.VMEM((M, D_MODEL), BF16), pltpu.VMEM((M, D_MODEL), BF16),
                        pltpu.VMEM((tm, D_MODEL), BF16)],
        compiler_params=_params("parallel", "arbitrary"),
        name="xattn_ln2",
    )(x3d, mem, w_kv, w_q, w_o, g, b)


def _ffn_kernel(x_ref, win_ref, wout_ref, g_ref, b_ref, o_ref, hid_ref):
    for sub in range(FFN_ROWS // FFN_SUB_ROWS):
        rows = slice(sub * FFN_SUB_ROWS, (sub + 1) * FFN_SUB_ROWS)
        x = x_ref[rows, :]
        xb = x.astype(BF16)
        for c in range(FFN_HIDDEN // FFN_CHUNK):
            lo = c * FFN_CHUNK
            a = jnp.dot(xb, win_ref[:, lo:lo + FFN_CHUNK], preferred_element_type=F32)
            gte = jnp.dot(xb, win_ref[:, FFN_HIDDEN + lo:FFN_HIDDEN + lo + FFN_CHUNK],
                          preferred_element_type=F32)
            hid_ref[rows, lo:lo + FFN_CHUNK] = (a * _sigmoid(a) * gte).astype(BF16)
        ff = jnp.dot(hid_ref[rows, :], wout_ref[...], preferred_element_type=F32)
        o_ref[rows, :] = _residual_norm(x, ff, g_ref[...], b_ref[...])


def _ffn_ln3(x2d, w_in, w_out, g, b):
    T = x2d.shape[0]
    tm = FFN_ROWS
    row = pl.BlockSpec((tm, D_MODEL), lambda i: (i, 0))
    return pl.pallas_call(
        _ffn_kernel,
        grid=(T // tm,),
        in_specs=[row, _resident(w_in.shape), _resident(w_out.shape),
                  _resident(g.shape), _resident(b.shape)],
        out_specs=row,
        out_shape=jax.ShapeDtypeStruct((T, D_MODEL), F32),
        scratch_shapes=[pltpu.VMEM((tm, FFN_HIDDEN), BF16)],
        compiler_params=_params("parallel"),
        name="ffn_ln3",
    )(x2d, w_in, w_out, g, b)


def _rope_tables(seq):
    half = RET_QK_DIM // 2
    inv = 1.0 / (ROPE_BASE ** (np.arange(half, dtype=np.float64) / half))
    ang = np.arange(seq, dtype=np.float64)[:, None] * inv[None, :]
    cos, sin = np.cos(ang), np.sin(ang)
    return (jnp.asarray(np.concatenate([cos, cos], axis=-1), F32),
            jnp.asarray(np.concatenate([-sin, sin], axis=-1), F32))


def _cumsum_matrices():
    t = SB_TILE
    r = np.arange(t)
    upper = (r[:, None] > r[None, :]).astype(np.float32)
    ones = np.ones((t, t), np.float32)
    zero = np.zeros((t, t), np.float32)
    half = np.concatenate([upper, ones], axis=1)
    cum = np.concatenate([half, half], axis=0)
    both = lambda m: np.concatenate([np.concatenate([m, zero], axis=1),
                                     np.concatenate([zero, m], axis=1)], axis=0)
    cum3 = np.concatenate([both(upper)] + [both(ones)] * 2, axis=0)
    return jnp.asarray(cum, BF16), jnp.asarray(cum3, BF16)


def kernel(x, mem, w_in, b_gate, w_sb_o, w_ret_o, w_mix_o, ln1_g, ln1_b, w_mem_q, w_mem_kv,
           w_mem_o, ln2_g, ln2_b, w_ffn_in, w_ffn_out, ln3_g, ln3_b):
    B, S, D = x.shape
    assert D == D_MODEL and w_in.shape == (DEPTH, D_MODEL, IN_WIDTH)
    assert S % RET_ROWS == 0 and (B * S) % ROW_TILE == 0 and S % ROW_TILE == 0
    cos_t, sin_t = _rope_tables(S)
    cum, cum3 = _cumsum_matrices()
    ret_tables = _retention_tables()
    x2d = x.reshape(B * S, D)
    for l in range(DEPTH):
        bf = lambda w: w[l].astype(BF16)
        bf_over_alpha = lambda w: (w[l] * (1.0 / DN_ALPHA)).astype(BF16)
        vec = lambda p: p[l][None, :]
        sbq, sbk, sbv, rq, rk, rv, rg, gates = _inproj(
            x2d, bf(w_in), vec(b_gate), cos_t, sin_t, S)
        seq3 = lambda a: a.reshape(B, S, a.shape[-1])
        attn = _sb_attention(seq3(sbq), seq3(sbk), seq3(sbv), cum, cum3)
        retg = _retention(seq3(rq), seq3(rk), seq3(rv), seq3(rg), ret_tables)
        x1 = _mix_ln1(attn.reshape(B * S, SB_WIDTH), retg.reshape(B * S, RET_V_WIDTH), gates, x2d,
                      bf(w_sb_o), bf(w_ret_o), bf_over_alpha(w_mix_o), vec(ln1_g), vec(ln1_b))
        w_q_scaled = (w_mem_q[l] * (MEM_HEAD_DIM ** -0.5 * LOG2_E)).astype(BF16)
        x2 = _xattn_ln2(x1.reshape(B, S, D), mem, bf(w_mem_kv), w_q_scaled, bf_over_alpha(w_mem_o),
                        vec(ln2_g), vec(ln2_b))
        x2d = _ffn_ln3(x2.reshape(B * S, D), bf(w_ffn_in), bf_over_alpha(w_ffn_out), vec(ln3_g), vec(ln3_b))
    return x2d.reshape(B, S, D)
```

```python
import functools

import jax
import jax.numpy as jnp
import numpy as np
from jax import lax
from jax.experimental import pallas as pl
from jax.experimental.pallas import tpu as pltpu

F32 = jnp.float32
BF16 = jnp.bfloat16

D_MODEL = 1024
DEPTH = 1
SB_HEADS = 8
SB_HEAD_DIM = 64
SB_WIDTH = SB_HEADS * SB_HEAD_DIM
RET_HEADS = 4
RET_QK_DIM = 128
RET_V_DIM = 256
RET_QK_WIDTH = RET_HEADS * RET_QK_DIM
RET_V_WIDTH = RET_HEADS * RET_V_DIM
ROPE_BASE = 10000.0
N_BRANCHES = 2
OFF_SB_Q = 0
OFF_SB_K = OFF_SB_Q + SB_WIDTH
OFF_SB_V = OFF_SB_K + SB_WIDTH
OFF_RET_Q = OFF_SB_V + SB_WIDTH
OFF_RET_K = OFF_RET_Q + RET_QK_WIDTH
OFF_RET_V = OFF_RET_K + RET_QK_WIDTH
OFF_RET_G = OFF_RET_V + RET_V_WIDTH
OFF_GATE = OFF_RET_G + RET_V_WIDTH
IN_WIDTH = OFF_GATE + N_BRANCHES * D_MODEL
MEM_HEADS = 4
MEM_HEAD_DIM = D_MODEL // MEM_HEADS
FFN_HIDDEN = 2816
DN_ALPHA = (2.0 * DEPTH) ** 0.25
LN_EPS = 1e-5

LANES = 128
VMEM_LIMIT_BYTES = 56 * 1024 * 1024
ROW_TILE = 512
SUB_ROWS = 256
DENSE_SUBTILES = 4
XATTN_SUBTILES = 4
SB_TILE = 128
SB_QTILES = 16
SB_GROUP = 8
SB_FAST_PHASES = 5
SB_TOP_ROWS = 32
SB_FIRST_STEPS = 3
SB_STEPS_PER_CHECK = 2
SB_UNDERFLOW = -150.5
LOG2_E = 1.4426950408889634
RET_CHUNK = 256
RET_ROWS = 4096
RET_LAG = 2
FFN_CHUNK = 256
FFN_ROWS = 1024
FFN_SUB_ROWS = 256


def _resident(shape):
    zeros = (0,) * len(shape)
    return pl.BlockSpec(shape, lambda *_: zeros, pipeline_mode=pl.Buffered(1))


def _params(*sem):
    return pltpu.CompilerParams(dimension_semantics=sem, vmem_limit_bytes=VMEM_LIMIT_BYTES)


def _run_staggered(groups, n_phases):
    results = [None] * len(groups)
    for slot in range(n_phases + len(groups) - 1):
        for g, group in enumerate(groups):
            if 0 <= slot - g < n_phases:
                results[g] = next(group)
    return results


def _sigmoid(x):
    return 0.5 * jnp.tanh(0.5 * x) + 0.5


def _residual_norm(x, branch_over_alpha, g, b):
    y = x + branch_over_alpha
    mu = jnp.mean(y, axis=-1, keepdims=True)
    yc = y - mu
    var = jnp.mean(yc * yc, axis=-1, keepdims=True)
    return yc * lax.rsqrt(var + LN_EPS / (DN_ALPHA * DN_ALPHA)) * g + b


def _inproj_kernel(x_ref, w_ref, bg_ref, cos_ref, sin_ref,
                   sbq_ref, sbk_ref, sbv_ref, rq_ref, rk_ref, rv_ref, rg_ref, gate_ref):
    xb = x_ref[...].astype(BF16)

    def proj(off, width):
        return jnp.dot(xb, w_ref[:, off:off + width], preferred_element_type=F32)

    for br in range(N_BRANCHES):
        sl = slice(br * D_MODEL, (br + 1) * D_MODEL)
        hg = proj(OFF_GATE + br * D_MODEL, D_MODEL) + bg_ref[:, sl]
        gate_ref[:, sl] = _sigmoid(hg).astype(BF16)
    g = proj(OFF_RET_G, RET_V_WIDTH)
    rg_ref[...] = (g * _sigmoid(g)).astype(BF16)

    def store_head_split(h, out_ref):
        hb = h.astype(BF16)
        first = lax.broadcasted_iota(jnp.int32, (hb.shape[0], LANES), 1) < SB_HEAD_DIM
        zero = jnp.zeros((hb.shape[0], LANES), BF16)
        for p in range(SB_WIDTH // LANES):
            pair = hb[:, p * LANES:(p + 1) * LANES]
            out_ref[:, 2 * p * LANES:(2 * p + 1) * LANES] = jnp.where(first, pair, zero)
            out_ref[:, (2 * p + 1) * LANES:(2 * p + 2) * LANES] = jnp.where(first, zero, pair)

    store_head_split(proj(OFF_SB_K, SB_WIDTH), sbk_ref)
    store_head_split(proj(OFF_SB_V, SB_WIDTH), sbv_ref)

    cos = cos_ref[...]
    sin = sin_ref[...]

    def rope_store(h, out_ref, scale):
        for hd in range(RET_HEADS):
            sl = slice(hd * RET_QK_DIM, (hd + 1) * RET_QK_DIM)
            xh = h[:, sl]
            rot = xh * cos + pltpu.roll(xh, RET_QK_DIM // 2, 1) * sin
            if scale != 1.0:
                rot = rot * scale
            out_ref[:, sl] = rot.astype(BF16)

    rope_store(proj(OFF_RET_Q, RET_QK_WIDTH), rq_ref, RET_QK_DIM ** -0.5)
    rope_store(proj(OFF_RET_K, RET_QK_WIDTH), rk_ref, 1.0)
    sbq_ref[...] = (proj(OFF_SB_Q, SB_WIDTH) * (SB_HEAD_DIM ** -0.5 * LOG2_E)).astype(BF16)
    rv_ref[...] = proj(OFF_RET_V, RET_V_WIDTH).astype(BF16)


def _inproj(x2d, w_in, b_gate, cos_t, sin_t, seq):
    T = x2d.shape[0]
    tm = ROW_TILE
    tiles_per_seq = seq // tm
    row = lambda w: pl.BlockSpec((tm, w), lambda i: (i, 0))
    tab = pl.BlockSpec((tm, LANES), lambda i: (i % tiles_per_seq, 0))
    widths = (SB_WIDTH, 2 * SB_WIDTH, 2 * SB_WIDTH, RET_QK_WIDTH, RET_QK_WIDTH,
              RET_V_WIDTH, RET_V_WIDTH, N_BRANCHES * D_MODEL)
    return pl.pallas_call(
        _inproj_kernel,
        grid=(T // tm,),
        in_specs=[row(D_MODEL), _resident((D_MODEL, IN_WIDTH)),
                  _resident((1, N_BRANCHES * D_MODEL)), tab, tab],
        out_specs=[row(w) for w in widths],
        out_shape=[jax.ShapeDtypeStruct((T, w), BF16) for w in widths],
        compiler_params=_params("parallel"),
        name="inproj",
    )(x2d, w_in, b_gate, cos_t, sin_t)


def _sb_kernel(q_ref, k_ref, v_ref, cum_ref, cum3_ref, o_ref):
    blk = pl.program_id(2)
    t = SB_TILE
    cum = cum_ref[...]
    row = lax.broadcasted_iota(jnp.int32, (t, 2 * t), 0)
    col = lax.broadcasted_iota(jnp.int32, (t, 2 * t), 1)
    causal = jnp.where(col >= t, col - t, col) < row

    def stack(x):
        return jnp.concatenate([x[:, :LANES], x[:, LANES:]], axis=0)

    def kv(j):
        start = pl.multiple_of(j * t, t)
        return stack(k_ref[0, pl.ds(start, t), :]), stack(v_ref[0, pl.ds(start, t), :])

    def q_tile(u):
        return q_ref[0, u * t:(u + 1) * t, :]

    def scores(q_rows, kbd):
        return lax.dot_general(q_rows, kbd, (((1,), (1,)), ((), ())), preferred_element_type=F32)

    def logs(y, mask):
        log_beta = jnp.minimum(y, 0.0) - jnp.log2(1.0 + jnp.exp2(-jnp.abs(y)))
        log_rem = log_beta - y
        if mask is not None:
            log_rem = jnp.where(mask, log_rem, 0.0)
        hi = log_rem.astype(BF16)
        lo = (log_rem - hi.astype(F32)).astype(BF16)
        rows = [jnp.concatenate([hi[:, h * t:(h + 1) * t], lo[:, h * t:(h + 1) * t]], axis=1)
                for h in range(2)]
        return log_beta, jnp.concatenate(rows, axis=0)

    def suffix_sums(operands):
        sums = jnp.dot(jnp.concatenate(operands, axis=0), cum, preferred_element_type=F32)
        out = []
        for n in range(len(operands)):
            a = sums[2 * n * t:(2 * n + 1) * t]
            b = sums[(2 * n + 1) * t:(2 * n + 2) * t]
            out.append((jnp.concatenate([a[:, :t], b[:, :t]], axis=1),
                        jnp.concatenate([a[:, t:], b[:, t:]], axis=1)))
        return out

    base = blk * SB_QTILES

    def fast_logs(y, mask):
        log_beta = jnp.minimum(y, 0.0) - jnp.log2(1.0 + jnp.exp2(-jnp.abs(y)))
        log_rem = log_beta - y
        if mask is not None:
            log_rem = jnp.where(mask, log_rem, 0.0)
        return log_beta, log_rem.astype(BF16)

    def fast(chains):
        top = SB_TOP_ROWS
        nrows = (t, t, top)
        offsets = range(chains[0] + 1 - len(nrows), chains[-1] + 1)
        members = {d: [(d + st, st) for st in range(len(nrows)) if d + st in chains]
                   for d in offsets}
        y, values = {}, {}
        for d in offsets:
            j = base + d
            kbd, vbd = kv(jnp.maximum(j, 0))
            if d < 0:
                vbd = jnp.where(j >= 0, vbd, jnp.zeros_like(vbd))
            values[d] = vbd
            q_rows = [q_ref[0, u * t:u * t + nrows[st], :] for u, st in members[d]]
            stacked = scores(jnp.concatenate(q_rows, axis=0), kbd)
            off = 0
            for u, st in members[d]:
                y[(u, st)] = stacked[off:off + nrows[st]]
                off += nrows[st]
        yield
        lb, hi = {}, {}
        for st in range(len(nrows)):
            for u in chains:
                lb[(u, st)], hi[(u, st)] = fast_logs(y[(u, st)], causal if st == 0 else None)
        yield
        cum3 = cum3_ref[...]

        def later(st):
            n = nrows[st]
            ops = [jnp.concatenate([hi[(u, st - i)][:n] for i in range(st + 1)], axis=1)
                   for u in chains]
            out = jnp.dot(jnp.concatenate(ops, axis=0), cum3[:2 * t * (st + 1)],
                          preferred_element_type=F32)
            return {u: out[i * n:(i + 1) * n] for i, u in enumerate(chains)}

        sums = [later(st) for st in range(len(nrows))]
        yield
        w = {}
        for st in range(len(nrows)):
            for u in chains:
                e = jnp.exp2(lb[(u, st)] + sums[st][u])
                w[(u, st)] = (jnp.where(causal, e, 0.0) if st == 0 else e).astype(BF16)
        yield
        acc, acc_top = {u: None for u in chains}, {}
        for d in offsets:
            ctx = jnp.dot(jnp.concatenate([w[key] for key in members[d]], axis=0), values[d],
                          preferred_element_type=F32)
            off = 0
            for u, st in members[d]:
                part = ctx[off:off + nrows[st]]
                off += nrows[st]
                if st == len(nrows) - 1:
                    acc_top[u] = part
                else:
                    acc[u] = part if acc[u] is None else acc[u] + part
        first_cols = lambda a: jnp.maximum(a[:, 0:1], a[:, t:t + 1])
        bound = None
        for u in chains:
            b = jnp.maximum(jnp.max(first_cols(sums[2][u])), jnp.max(first_cols(sums[1][u][top:])))
            bound = b if bound is None else jnp.maximum(bound, b)
        yield bound, acc, acc_top

    def walk(s, nsteps, run, acc):
        from_diag = isinstance(s, int) and s == 0
        steps = range(nsteps)
        offsets = range(1 - nsteps, SB_QTILES)
        members = {d: [(u, st) for st in steps for u in range(SB_QTILES) if u - st == d]
                   for d in offsets}
        order = [(u, st) for st in steps for u in range(SB_QTILES)]
        is_diag = lambda st: from_diag and st == 0
        y, values = {}, {}
        for d in offsets:
            j = base + d - s
            kbd, vbd = kv(jnp.maximum(j, 0))
            if not (from_diag and d >= 0):
                vbd = jnp.where(j >= 0, vbd, jnp.zeros_like(vbd))
            values[d] = vbd
            stacked = scores(jnp.concatenate([q_tile(u) for u, _ in members[d]], axis=0), kbd)
            for n, key in enumerate(members[d]):
                y[key] = stacked[n * t:(n + 1) * t]
        parts = {key: logs(y[key], causal if is_diag(key[1]) else None) for key in order}
        sums = []
        for st in steps:
            sums.extend(suffix_sums([parts[key][1] for key in order if key[1] == st]))
        run, acc, w = list(run), list(acc), {}
        for key, (later, rowsum) in zip(order, sums):
            u = key[0]
            if is_diag(key[1]):
                w[key] = jnp.where(causal, jnp.exp2(parts[key][0] + later), 0.0).astype(BF16)
                run[u] = rowsum
            else:
                w[key] = jnp.exp2(parts[key][0] + later + run[u]).astype(BF16)
                run[u] = run[u] + rowsum
        for d in offsets:
            ctx = jnp.dot(jnp.concatenate([w[key] for key in members[d]], axis=0), values[d],
                          preferred_element_type=F32)
            for n, (u, _) in enumerate(members[d]):
                part = ctx[n * t:(n + 1) * t]
                acc[u] = part if acc[u] is None else acc[u] + part
        top = run[0]
        for u in range(1, SB_QTILES):
            top = jnp.maximum(top, run[u])
        return jnp.max(top), run, acc

    def general():
        top, run, acc = walk(0, SB_FIRST_STEPS, [None] * SB_QTILES, [None] * SB_QTILES)

        def live(carry):
            return jnp.logical_and(carry[1] > SB_UNDERFLOW, carry[0] <= base + SB_QTILES - 1)

        def body(carry):
            s = carry[0]
            top, run, acc = walk(s, SB_STEPS_PER_CHECK, carry[2:2 + SB_QTILES],
                                 carry[2 + SB_QTILES:])
            return (s + SB_STEPS_PER_CHECK, top, *run, *acc)

        out = lax.while_loop(live, body, (jnp.int32(SB_FIRST_STEPS), top, *run, *acc))
        for u in range(SB_QTILES):
            o_ref[0, u * t:(u + 1) * t, :] = out[2 + SB_QTILES + u].astype(BF16)

    groups = [fast(range(first, first + SB_GROUP)) for first in range(0, SB_QTILES, SB_GROUP)]
    results = _run_staggered(groups, SB_FAST_PHASES)
    bound, acc, acc_top = None, {}, {}
    for b, a, a_top in results:
        bound = b if bound is None else jnp.maximum(bound, b)
        acc.update(a)
        acc_top.update(a_top)
    finished = bound < SB_UNDERFLOW

    @pl.when(finished)
    def _():
        for u in range(SB_QTILES):
            lo_rows = u * t + SB_TOP_ROWS
            o_ref[0, u * t:lo_rows, :] = (acc[u][:SB_TOP_ROWS] + acc_top[u]).astype(BF16)
            o_ref[0, lo_rows:(u + 1) * t, :] = acc[u][SB_TOP_ROWS:].astype(BF16)

    @pl.when(jnp.logical_not(finished))
    def _():
        general()


def _sb_attention(q, k, v, cum, cum3):
    B, S, _ = q.shape
    rows = SB_TILE * SB_QTILES
    pairs = SB_WIDTH // LANES
    qspec = pl.BlockSpec((1, rows, LANES), lambda b, p, i: (b, i, p))
    kvspec = pl.BlockSpec((1, S, 2 * LANES), lambda b, p, i: (b, 0, p))
    return pl.pallas_call(
        _sb_kernel,
        grid=(B, pairs, S // rows),
        in_specs=[qspec, kvspec, kvspec, _resident(cum.shape), _resident(cum3.shape)],
        out_specs=qspec,
        out_shape=jax.ShapeDtypeStruct((B, S, SB_WIDTH), BF16),
        compiler_params=_params("parallel", "parallel", "arbitrary"),
        name="sb_attn",
    )(q, k, v, cum, cum3)


def _ret_kernel(q_ref, k_ref, v_ref, g_ref, decay_ref, qd_ref, kd_ref, cd_ref, o_ref, state_ref):
    @pl.when(pl.program_id(2) == 0)
    def _():
        state_ref[...] = jnp.zeros_like(state_ref)

    c = RET_CHUNK
    carried = [state_ref[...]]

    def chunk(sl):
        q = q_ref[0, sl, :]
        k = k_ref[0, sl, :]
        v = v_ref[0, sl, :]
        s = lax.dot_general(q, k, (((1,), (1,)), ((), ())), preferred_element_type=F32) * decay_ref[0]
        inner = jnp.dot(s.astype(BF16), v, preferred_element_type=F32)
        kdec = (k.astype(F32) * kd_ref[0]).astype(BF16)
        kv = lax.dot_general(kdec, v, (((0,), (0,)), ((), ())), preferred_element_type=F32)
        yield
        for _ in range(RET_LAG - 1):
            yield
        state = carried[0]
        carried[0] = state * cd_ref[0] + kv
        o = inner + jnp.dot(q, state.astype(BF16), preferred_element_type=F32) * qd_ref[0]
        mu = jnp.mean(o, axis=-1, keepdims=True)
        oc = o - mu
        var = jnp.mean(oc * oc, axis=-1, keepdims=True)
        normed = oc * lax.rsqrt(var + LN_EPS)
        o_ref[0, sl, :] = normed.astype(BF16) * g_ref[0, sl, :]
        yield

    _run_staggered([chunk(slice(ci * c, (ci + 1) * c)) for ci in range(RET_ROWS // c)], RET_LAG + 1)
    state_ref[...] = carried[0]


def _retention(rq, rk, rv, rg, tables):
    B, S, _ = rq.shape
    c = RET_CHUNK
    qk = pl.BlockSpec((1, RET_ROWS, RET_QK_DIM), lambda b, h, r: (b, r, h))
    vv = pl.BlockSpec((1, RET_ROWS, RET_V_DIM), lambda b, h, r: (b, r, h))
    head = lambda rows, cols: pl.BlockSpec((1, rows, cols), lambda b, h, r: (h, 0, 0))
    return pl.pallas_call(
        _ret_kernel,
        grid=(B, RET_HEADS, S // RET_ROWS),
        in_specs=[qk, qk, vv, vv, head(c, c), head(c, RET_V_DIM), head(c, RET_QK_DIM),
                  head(RET_QK_DIM, RET_V_DIM)],
        out_specs=vv,
        out_shape=jax.ShapeDtypeStruct((B, S, RET_V_WIDTH), BF16),
        scratch_shapes=[pltpu.VMEM((RET_QK_DIM, RET_V_DIM), F32)],
        compiler_params=_params("parallel", "parallel", "arbitrary"),
        name="retention",
    )(rq, rk, rv, rg, *tables)


def _retention_tables():
    c = RET_CHUNK
    log_gamma = np.log1p(-np.exp2(-5.0 - np.arange(RET_HEADS, dtype=np.float64)))
    idx = np.arange(c, dtype=np.float64)
    rel = idx[:, None] - idx[None, :]
    lg = log_gamma[:, None, None]
    decay = np.where(rel[None] >= 0, np.exp(lg * np.maximum(rel, 0.0)[None]), 0.0)
    qd = np.broadcast_to(np.exp(lg * (idx + 1.0)[None, :, None]), (RET_HEADS, c, RET_V_DIM))
    kd = np.broadcast_to(np.exp(lg * (c - 1.0 - idx)[None, :, None]), (RET_HEADS, c, RET_QK_DIM))
    cd = np.broadcast_to(np.exp(lg * float(c)), (RET_HEADS, RET_QK_DIM, RET_V_DIM))
    return tuple(jnp.asarray(t, F32) for t in (decay, qd, kd, cd))


def _mix_kernel(a_ref, r_ref, gate_ref, x_ref, wsb_ref, wret_ref, wmix_ref, g_ref, b_ref, o_ref):
    for sub in range(DENSE_SUBTILES):
        rows = slice(sub * SUB_ROWS, (sub + 1) * SUB_ROWS)
        y_sb = jnp.dot(a_ref[rows, :], wsb_ref[...], preferred_element_type=F32)
        y_ret = jnp.dot(r_ref[rows, :], wret_ref[...], preferred_element_type=F32)
        merged = (gate_ref[rows, :D_MODEL] * y_sb.astype(BF16)
                  + gate_ref[rows, D_MODEL:] * y_ret.astype(BF16))
        mix = jnp.dot(merged, wmix_ref[...], preferred_element_type=F32)
        o_ref[rows, :] = _residual_norm(x_ref[rows, :], mix, g_ref[...], b_ref[...])


def _mix_ln1(attn, retg, gates, x2d, w_sb_o, w_ret_o, w_mix_o, g, b):
    T = x2d.shape[0]
    tm = SUB_ROWS * DENSE_SUBTILES
    row = lambda w: pl.BlockSpec((tm, w), lambda i: (i, 0))
    return pl.pallas_call(
        _mix_kernel,
        grid=(T // tm,),
        in_specs=[row(SB_WIDTH), row(RET_V_WIDTH), row(N_BRANCHES * D_MODEL), row(D_MODEL),
                  _resident(w_sb_o.shape), _resident(w_ret_o.shape), _resident(w_mix_o.shape),
                  _resident(g.shape), _resident(b.shape)],
        out_specs=row(D_MODEL),
        out_shape=jax.ShapeDtypeStruct((T, D_MODEL), F32),
        compiler_params=_params("parallel"),
        name="mix_ln1",
    )(attn, retg, gates, x2d, w_sb_o, w_ret_o, w_mix_o, g, b)


def _xattn_kernel(x_ref, m_ref, wkv_ref, wq_ref, wo_ref, g_ref, b_ref, o_ref, k_ref, v_ref, ctx_ref):
    @pl.when(pl.program_id(1) == 0)
    def _():
        mb = m_ref[0].astype(BF16)
        k_ref[...] = jnp.dot(mb, wkv_ref[:, :D_MODEL], preferred_element_type=F32).astype(BF16)
        v_ref[...] = jnp.dot(mb, wkv_ref[:, D_MODEL:], preferred_element_type=F32).astype(BF16)

    def group(sub):
        rows = slice(sub * SUB_ROWS, (sub + 1) * SUB_ROWS)
        x = x_ref[0, rows, :]
        q = jnp.dot(x.astype(BF16), wq_ref[...], preferred_element_type=F32).astype(BF16)
        yield
        for h in range(MEM_HEADS):
            sl = slice(h * MEM_HEAD_DIM, (h + 1) * MEM_HEAD_DIM)
            s = lax.dot_general(q[:, sl], k_ref[:, sl], (((1,), (1,)), ((), ())),
                                preferred_element_type=F32)
            p = jnp.exp2(s - jnp.max(s, axis=-1, keepdims=True))
            denom = jnp.sum(p, axis=-1, keepdims=True)
            ctx = jnp.dot(p.astype(BF16), v_ref[:, sl], preferred_element_type=F32)
            ctx_ref[rows, sl] = (ctx / denom).astype(BF16)
        yield
        xa = jnp.dot(ctx_ref[rows, :], wo_ref[...], preferred_element_type=F32)
        yield
        o_ref[0, rows, :] = _residual_norm(x, xa, g_ref[...], b_ref[...])
        yield

    _run_staggered([group(sub) for sub in range(XATTN_SUBTILES)], 4)


def _xattn_ln2(x3d, mem, w_kv, w_q, w_o, g, b):
    B, S, _ = x3d.shape
    M = mem.shape[1]
    tm = SUB_ROWS * XATTN_SUBTILES
    row = pl.BlockSpec((1, tm, D_MODEL), lambda bi, i: (bi, i, 0))
    mem_blk = pl.BlockSpec((1, M, D_MODEL), lambda bi, i: (bi, 0, 0))
    return pl.pallas_call(
        _xattn_kernel,
        grid=(B, S // tm),
        in_specs=[row, mem_blk, _resident(w_kv.shape), _resident(w_q.shape), _resident(w_o.shape),
                  _resident(g.shape), _resident(b.shape)],
        out_specs=row,
        out_shape=jax.ShapeDtypeStruct((B, S, D_MODEL), F32),
        scratch_shapes=[pltpu.VMEM((M, D_MODEL), BF16), pltpu.VMEM((M, D_MODEL), BF16),
                        pltpu.VMEM((tm, D_MODEL), BF16)],
        compiler_params=_params("parallel", "arbitrary"),
        name="xattn_ln2",
    )(x3d, mem, w_kv, w_q, w_o, g, b)


def _ffn_kernel(x_ref, win_ref, wout_ref, g_ref, b_ref, o_ref, hid_ref):
    for sub in range(FFN_ROWS // FFN_SUB_ROWS):
        rows = slice(sub * FFN_SUB_ROWS, (sub + 1) * FFN_SUB_ROWS)
        x = x_ref[rows, :]
        xb = x.astype(BF16)
        for c in range(FFN_HIDDEN // FFN_CHUNK):
            lo = c * FFN_CHUNK
            a = jnp.dot(xb, win_ref[:, lo:lo + FFN_CHUNK], preferred_element_type=F32)
            gte = jnp.dot(xb, win_ref[:, FFN_HIDDEN + lo:FFN_HIDDEN + lo + FFN_CHUNK],
                          preferred_element_type=F32)
            hid_ref[rows, lo:lo + FFN_CHUNK] = (a * _sigmoid(a) * gte).astype(BF16)
        ff = jnp.dot(hid_ref[rows, :], wout_ref[...], preferred_element_type=F32)
        o_ref[rows, :] = _residual_norm(x, ff, g_ref[...], b_ref[...])


def _ffn_ln3(x2d, w_in, w_out, g, b):
    T = x2d.shape[0]
    tm = FFN_ROWS
    row = pl.BlockSpec((tm, D_MODEL), lambda i: (i, 0))
    return pl.pallas_call(
        _ffn_kernel,
        grid=(T // tm,),
        in_specs=[row, _resident(w_in.shape), _resident(w_out.shape),
                  _resident(g.shape), _resident(b.shape)],
        out_specs=row,
        out_shape=jax.ShapeDtypeStruct((T, D_MODEL), F32),
        scratch_shapes=[pltpu.VMEM((tm, FFN_HIDDEN), BF16)],
        compiler_params=_params("parallel"),
        name="ffn_ln3",
    )(x2d, w_in, w_out, g, b)


def _rope_tables(seq):
    half = RET_QK_DIM // 2
    inv = 1.0 / (ROPE_BASE ** (np.arange(half, dtype=np.float64) / half))
    ang = np.arange(seq, dtype=np.float64)[:, None] * inv[None, :]
    cos, sin = np.cos(ang), np.sin(ang)
    return (jnp.asarray(np.concatenate([cos, cos], axis=-1), F32),
            jnp.asarray(np.concatenate([-sin, sin], axis=-1), F32))


def _cumsum_matrices():
    t = SB_TILE
    r = np.arange(t)
    upper = (r[:, None] > r[None, :]).astype(np.float32)
    ones = np.ones((t, t), np.float32)
    zero = np.zeros((t, t), np.float32)
    half = np.concatenate([upper, ones], axis=1)
    cum = np.concatenate([half, half], axis=0)
    both = lambda m: np.concatenate([np.concatenate([m, zero], axis=1),
                                     np.concatenate([zero, m], axis=1)], axis=0)
    cum3 = np.concatenate([both(upper)] + [both(ones)] * 2, axis=0)
    return jnp.asarray(cum, BF16), jnp.asarray(cum3, BF16)


def kernel(x, mem, w_in, b_gate, w_sb_o, w_ret_o, w_mix_o, ln1_g, ln1_b, w_mem_q, w_mem_kv,
           w_mem_o, ln2_g, ln2_b, w_ffn_in, w_ffn_out, ln3_g, ln3_b):
    B, S, D = x.shape
    assert D == D_MODEL and w_in.shape == (DEPTH, D_MODEL, IN_WIDTH)
    assert S % RET_ROWS == 0 and (B * S) % ROW_TILE == 0 and S % ROW_TILE == 0
    cos_t, sin_t = _rope_tables(S)
    cum, cum3 = _cumsum_matrices()
    ret_tables = _retention_tables()
    x2d = x.reshape(B * S, D)
    for l in range(DEPTH):
        bf = lambda w: w[l].astype(BF16)
        bf_over_alpha = lambda w: (w[l] * (1.0 / DN_ALPHA)).astype(BF16)
        vec = lambda p: p[l][None, :]
        sbq, sbk, sbv, rq, rk, rv, rg, gates = _inproj(
            x2d, bf(w_in), vec(b_gate), cos_t, sin_t, S)
        seq3 = lambda a: a.reshape(B, S, a.shape[-1])
        attn = _sb_attention(seq3(sbq), seq3(sbk), seq3(sbv), cum, cum3)
        retg = _retention(seq3(rq), seq3(rk), seq3(rv), seq3(rg), ret_tables)
        x1 = _mix_ln1(attn.reshape(B * S, SB_WIDTH), retg.reshape(B * S, RET_V_WIDTH), gates, x2d,
                      bf(w_sb_o), bf(w_ret_o), bf_over_alpha(w_mix_o), vec(ln1_g), vec(ln1_b))
        w_q_scaled = (w_mem_q[l] * (MEM_HEAD_DIM ** -0.5 * LOG2_E)).astype(BF16)
        x2 = _xattn_ln2(x1.reshape(B, S, D), mem, bf(w_mem_kv), w_q_scaled, bf_over_alpha(w_mem_o),
                        vec(ln2_g), vec(ln2_b))
        x2d = _ffn_ln3(x2.reshape(B * S, D), bf(w_ffn_in), bf_over_alpha(w_ffn_out), vec(ln3_g), vec(ln3_b))
    return x2d.reshape(B, S, D)
```

```python
import jax
import jax.numpy as jnp
import numpy as np
from jax import lax
from jax.experimental import pallas as pl
from jax.experimental.pallas import tpu as pltpu

F32 = jnp.float32
BF16 = jnp.bfloat16

D_MODEL = 1024
DEPTH = 1
SB_HEADS = 8
SB_HEAD_DIM = 64
SB_WIDTH = SB_HEADS * SB_HEAD_DIM
RET_HEADS = 4
RET_QK_DIM = 128
RET_V_DIM = 256
RET_QK_WIDTH = RET_HEADS * RET_QK_DIM
RET_V_WIDTH = RET_HEADS * RET_V_DIM
ROPE_BASE = 10000.0
N_BRANCHES = 2
OFF_SB_Q = 0
OFF_SB_K = OFF_SB_Q + SB_WIDTH
OFF_SB_V = OFF_SB_K + SB_WIDTH
OFF_RET_Q = OFF_SB_V + SB_WIDTH
OFF_RET_K = OFF_RET_Q + RET_QK_WIDTH
OFF_RET_V = OFF_RET_K + RET_QK_WIDTH
OFF_RET_G = OFF_RET_V + RET_V_WIDTH
OFF_GATE = OFF_RET_G + RET_V_WIDTH
IN_WIDTH = OFF_GATE + N_BRANCHES * D_MODEL
MEM_HEADS = 4
MEM_HEAD_DIM = D_MODEL // MEM_HEADS
FFN_HIDDEN = -(-8 * D_MODEL // (3 * 256)) * 256
DN_ALPHA = (2.0 * DEPTH) ** 0.25
LN_EPS = 1e-5

LANES = 128
V7X_VMEM_BYTES = 64 * 1024 * 1024
VMEM_LIMIT_BYTES = V7X_VMEM_BYTES * 7 // 8
ROW_TILE = 512
SUB_ROWS = 256
DENSE_SUBTILES = 4
XATTN_SUBTILES = 4
SB_TILE = 128
SB_QTILES = 16
SB_GROUP = 8
SB_FAST_PHASES = 5
SB_TOP_ROWS = 32
SB_FIRST_STEPS = 3
SB_STEPS_PER_CHECK = 2
SB_UNDERFLOW = -150.5
LOG2_E = 1.4426950408889634
RET_CHUNK = 256
RET_ROWS = 4096
RET_LAG = 2
FFN_CHUNK = 256
FFN_ROWS = 1024
FFN_SUB_ROWS = 256


def _resident(shape):
    zeros = (0,) * len(shape)
    return pl.BlockSpec(shape, lambda *_: zeros, pipeline_mode=pl.Buffered(1))


def _params(*sem):
    return pltpu.CompilerParams(dimension_semantics=sem, vmem_limit_bytes=VMEM_LIMIT_BYTES)


def _run_staggered(groups, n_phases):
    results = [None] * len(groups)
    for slot in range(n_phases + len(groups) - 1):
        for g, group in enumerate(groups):
            if 0 <= slot - g < n_phases:
                results[g] = next(group)
    return results


def _sigmoid(x):
    return 0.5 * jnp.tanh(0.5 * x) + 0.5


def _residual_norm(x, branch_over_alpha, g, b):
    y = x + branch_over_alpha
    mu = jnp.mean(y, axis=-1, keepdims=True)
    yc = y - mu
    var = jnp.mean(yc * yc, axis=-1, keepdims=True)
    return yc * lax.rsqrt(var + LN_EPS / (DN_ALPHA * DN_ALPHA)) * g + b


def _inproj_kernel(x_ref, w_ref, bg_ref, cos_ref, sin_ref,
                   sbq_ref, sbk_ref, sbv_ref, rq_ref, rk_ref, rv_ref, rg_ref, gate_ref):
    xb = x_ref[...].astype(BF16)

    def proj(off, width):
        return jnp.dot(xb, w_ref[:, off:off + width], preferred_element_type=F32)

    for br in range(N_BRANCHES):
        sl = slice(br * D_MODEL, (br + 1) * D_MODEL)
        hg = proj(OFF_GATE + br * D_MODEL, D_MODEL) + bg_ref[:, sl]
        gate_ref[:, sl] = _sigmoid(hg).astype(BF16)
    g = proj(OFF_RET_G, RET_V_WIDTH)
    rg_ref[...] = (g * _sigmoid(g)).astype(BF16)

    def store_head_split(h, out_ref):
        hb = h.astype(BF16)
        first = lax.broadcasted_iota(jnp.int32, (hb.shape[0], LANES), 1) < SB_HEAD_DIM
        zero = jnp.zeros((hb.shape[0], LANES), BF16)
        for p in range(SB_WIDTH // LANES):
            pair = hb[:, p * LANES:(p + 1) * LANES]
            out_ref[:, 2 * p * LANES:(2 * p + 1) * LANES] = jnp.where(first, pair, zero)
            out_ref[:, (2 * p + 1) * LANES:(2 * p + 2) * LANES] = jnp.where(first, zero, pair)

    store_head_split(proj(OFF_SB_K, SB_WIDTH), sbk_ref)
    store_head_split(proj(OFF_SB_V, SB_WIDTH), sbv_ref)

    cos = cos_ref[...]
    sin = sin_ref[...]

    def rope_store(h, out_ref, scale):
        for hd in range(RET_HEADS):
            sl = slice(hd * RET_QK_DIM, (hd + 1) * RET_QK_DIM)
            xh = h[:, sl]
            rot = xh * cos + pltpu.roll(xh, RET_QK_DIM // 2, 1) * sin
            if scale != 1.0:
                rot = rot * scale
            out_ref[:, sl] = rot.astype(BF16)

    rope_store(proj(OFF_RET_Q, RET_QK_WIDTH), rq_ref, RET_QK_DIM ** -0.5)
    rope_store(proj(OFF_RET_K, RET_QK_WIDTH), rk_ref, 1.0)
    sbq_ref[...] = (proj(OFF_SB_Q, SB_WIDTH) * (SB_HEAD_DIM ** -0.5 * LOG2_E)).astype(BF16)
    rv_ref[...] = proj(OFF_RET_V, RET_V_WIDTH).astype(BF16)


def _inproj(x2d, w_in, b_gate, cos_t, sin_t, seq):
    T = x2d.shape[0]
    tm = ROW_TILE
    tiles_per_seq = seq // tm
    row = lambda w: pl.BlockSpec((tm, w), lambda i: (i, 0))
    tab = pl.BlockSpec((tm, LANES), lambda i: (i % tiles_per_seq, 0))
    widths = (SB_WIDTH, 2 * SB_WIDTH, 2 * SB_WIDTH, RET_QK_WIDTH, RET_QK_WIDTH,
              RET_V_WIDTH, RET_V_WIDTH, N_BRANCHES * D_MODEL)
    return pl.pallas_call(
        _inproj_kernel,
        grid=(T // tm,),
        in_specs=[row(D_MODEL), _resident((D_MODEL, IN_WIDTH)),
                  _resident((1, N_BRANCHES * D_MODEL)), tab, tab],
        out_specs=[row(w) for w in widths],
        out_shape=[jax.ShapeDtypeStruct((T, w), BF16) for w in widths],
        compiler_params=_params("parallel"),
        name="inproj",
    )(x2d, w_in, b_gate, cos_t, sin_t)


def _sb_kernel(q_ref, k_ref, v_ref, cum_ref, cum3_ref, o_ref):
    blk = pl.program_id(2)
    t = SB_TILE
    cum = cum_ref[...]
    row = lax.broadcasted_iota(jnp.int32, (t, 2 * t), 0)
    col = lax.broadcasted_iota(jnp.int32, (t, 2 * t), 1)
    causal = jnp.where(col >= t, col - t, col) < row

    def stack(x):
        return jnp.concatenate([x[:, :LANES], x[:, LANES:]], axis=0)

    def kv(j):
        start = pl.multiple_of(j * t, t)
        return stack(k_ref[0, pl.ds(start, t), :]), stack(v_ref[0, pl.ds(start, t), :])

    def q_tile(u):
        return q_ref[0, u * t:(u + 1) * t, :]

    def scores(q_rows, kbd):
        return lax.dot_general(q_rows, kbd, (((1,), (1,)), ((), ())), preferred_element_type=F32)

    def logs(y, mask):
        log_beta = jnp.minimum(y, 0.0) - jnp.log2(1.0 + jnp.exp2(-jnp.abs(y)))
        log_rem = log_beta - y
        if mask is not None:
            log_rem = jnp.where(mask, log_rem, 0.0)
        hi = log_rem.astype(BF16)
        lo = (log_rem - hi.astype(F32)).astype(BF16)
        rows = [jnp.concatenate([hi[:, h * t:(h + 1) * t], lo[:, h * t:(h + 1) * t]], axis=1)
                for h in range(2)]
        return log_beta, jnp.concatenate(rows, axis=0)

    def suffix_sums(operands):
        sums = jnp.dot(jnp.concatenate(operands, axis=0), cum, preferred_element_type=F32)
        out = []
        for n in range(len(operands)):
            a = sums[2 * n * t:(2 * n + 1) * t]
            b = sums[(2 * n + 1) * t:(2 * n + 2) * t]
            out.append((jnp.concatenate([a[:, :t], b[:, :t]], axis=1),
                        jnp.concatenate([a[:, t:], b[:, t:]], axis=1)))
        return out

    base = blk * SB_QTILES

    def fast_logs(y, mask):
        log_beta = jnp.minimum(y, 0.0) - jnp.log2(1.0 + jnp.exp2(-jnp.abs(y)))
        log_rem = log_beta - y
        if mask is not None:
            log_rem = jnp.where(mask, log_rem, 0.0)
        return log_beta, log_rem.astype(BF16)

    def fast(chains):
        top = SB_TOP_ROWS
        nrows = (t, t, top)
        offsets = range(chains[0] + 1 - len(nrows), chains[-1] + 1)
        members = {d: [(d + st, st) for st in range(len(nrows)) if d + st in chains]
                   for d in offsets}
        y, values = {}, {}
        for d in offsets:
            j = base + d
            kbd, vbd = kv(jnp.maximum(j, 0))
            if d < 0:
                vbd = jnp.where(j >= 0, vbd, jnp.zeros_like(vbd))
            values[d] = vbd
            q_rows = [q_ref[0, u * t:u * t + nrows[st], :] for u, st in members[d]]
            stacked = scores(jnp.concatenate(q_rows, axis=0), kbd)
            off = 0
            for u, st in members[d]:
                y[(u, st)] = stacked[off:off + nrows[st]]
                off += nrows[st]
        yield
        lb, hi = {}, {}
        for st in range(len(nrows)):
            for u in chains:
                lb[(u, st)], hi[(u, st)] = fast_logs(y[(u, st)], causal if st == 0 else None)
        yield
        cum3 = cum3_ref[...]

        def later(st):
            n = nrows[st]
            ops = [jnp.concatenate([hi[(u, st - i)][:n] for i in range(st + 1)], axis=1)
                   for u in chains]
            out = jnp.dot(jnp.concatenate(ops, axis=0), cum3[:2 * t * (st + 1)],
                          preferred_element_type=F32)
            return {u: out[i * n:(i + 1) * n] for i, u in enumerate(chains)}

        sums = [later(st) for st in range(len(nrows))]
        yield
        w = {}
        for st in range(len(nrows)):
            for u in chains:
                e = jnp.exp2(lb[(u, st)] + sums[st][u])
                w[(u, st)] = (jnp.where(causal, e, 0.0) if st == 0 else e).astype(BF16)
        yield
        acc, acc_top = {u: None for u in chains}, {}
        for d in offsets:
            ctx = jnp.dot(jnp.concatenate([w[key] for key in members[d]], axis=0), values[d],
                          preferred_element_type=F32)
            off = 0
            for u, st in members[d]:
                part = ctx[off:off + nrows[st]]
                off += nrows[st]
                if st == len(nrows) - 1:
                    acc_top[u] = part
                else:
                    acc[u] = part if acc[u] is None else acc[u] + part
        first_cols = lambda a: jnp.maximum(a[:, 0:1], a[:, t:t + 1])
        bound = None
        for u in chains:
            b = jnp.maximum(jnp.max(first_cols(sums[2][u])), jnp.max(first_cols(sums[1][u][top:])))
            bound = b if bound is None else jnp.maximum(bound, b)
        yield bound, acc, acc_top

    def walk(s, nsteps, run, acc):
        from_diag = isinstance(s, int) and s == 0
        steps = range(nsteps)
        offsets = range(1 - nsteps, SB_QTILES)
        members = {d: [(u, st) for st in steps for u in range(SB_QTILES) if u - st == d]
                   for d in offsets}
        order = [(u, st) for st in steps for u in range(SB_QTILES)]
        is_diag = lambda st: from_diag and st == 0
        y, values = {}, {}
        for d in offsets:
            j = base + d - s
            kbd, vbd = kv(jnp.maximum(j, 0))
            if not (from_diag and d >= 0):
                vbd = jnp.where(j >= 0, vbd, jnp.zeros_like(vbd))
            values[d] = vbd
            stacked = scores(jnp.concatenate([q_tile(u) for u, _ in members[d]], axis=0), kbd)
            for n, key in enumerate(members[d]):
                y[key] = stacked[n * t:(n + 1) * t]
        parts = {key: logs(y[key], causal if is_diag(key[1]) else None) for key in order}
        sums = []
        for st in steps:
            sums.extend(suffix_sums([parts[key][1] for key in order if key[1] == st]))
        run, acc, w = list(run), list(acc), {}
        for key, (later, rowsum) in zip(order, sums):
            u = key[0]
            if is_diag(key[1]):
                w[key] = jnp.where(causal, jnp.exp2(parts[key][0] + later), 0.0).astype(BF16)
                run[u] = rowsum
            else:
                w[key] = jnp.exp2(parts[key][0] + later + run[u]).astype(BF16)
                run[u] = run[u] + rowsum
        for d in offsets:
            ctx = jnp.dot(jnp.concatenate([w[key] for key in members[d]], axis=0), values[d],
                          preferred_element_type=F32)
            for n, (u, _) in enumerate(members[d]):
                part = ctx[n * t:(n + 1) * t]
                acc[u] = part if acc[u] is None else acc[u] + part
        top = run[0]
        for u in range(1, SB_QTILES):
            top = jnp.maximum(top, run[u])
        return jnp.max(top), run, acc

    def general():
        top, run, acc = walk(0, SB_FIRST_STEPS, [None] * SB_QTILES, [None] * SB_QTILES)

        def live(carry):
            return jnp.logical_and(carry[1] > SB_UNDERFLOW, carry[0] <= base + SB_QTILES - 1)

        def body(carry):
            s = carry[0]
            top, run, acc = walk(s, SB_STEPS_PER_CHECK, carry[2:2 + SB_QTILES],
                                 carry[2 + SB_QTILES:])
            return (s + SB_STEPS_PER_CHECK, top, *run, *acc)

        out = lax.while_loop(live, body, (jnp.int32(SB_FIRST_STEPS), top, *run, *acc))
        for u in range(SB_QTILES):
            o_ref[0, u * t:(u + 1) * t, :] = out[2 + SB_QTILES + u].astype(BF16)

    groups = [fast(range(first, first + SB_GROUP)) for first in range(0, SB_QTILES, SB_GROUP)]
    results = _run_staggered(groups, SB_FAST_PHASES)
    bound, acc, acc_top = None, {}, {}
    for b, a, a_top in results:
        bound = b if bound is None else jnp.maximum(bound, b)
        acc.update(a)
        acc_top.update(a_top)
    finished = bound < SB_UNDERFLOW

    @pl.when(finished)
    def _():
        for u in range(SB_QTILES):
            lo_rows = u * t + SB_TOP_ROWS
            o_ref[0, u * t:lo_rows, :] = (acc[u][:SB_TOP_ROWS] + acc_top[u]).astype(BF16)
            o_ref[0, lo_rows:(u + 1) * t, :] = acc[u][SB_TOP_ROWS:].astype(BF16)

    @pl.when(jnp.logical_not(finished))
    def _():
        general()


def _sb_attention(q, k, v, cum, cum3):
    B, S, _ = q.shape
    rows = SB_TILE * SB_QTILES
    pairs = SB_WIDTH // LANES
    qspec = pl.BlockSpec((1, rows, LANES), lambda b, p, i: (b, i, p))
    kvspec = pl.BlockSpec((1, S, 2 * LANES), lambda b, p, i: (b, 0, p))
    return pl.pallas_call(
        _sb_kernel,
        grid=(B, pairs, S // rows),
        in_specs=[qspec, kvspec, kvspec, _resident(cum.shape), _resident(cum3.shape)],
        out_specs=qspec,
        out_shape=jax.ShapeDtypeStruct((B, S, SB_WIDTH), BF16),
        compiler_params=_params("parallel", "parallel", "arbitrary"),
        name="sb_attn",
    )(q, k, v, cum, cum3)


def _ret_kernel(q_ref, k_ref, v_ref, g_ref, decay_ref, qd_ref, kd_ref, cd_ref, o_ref, state_ref):
    @pl.when(pl.program_id(2) == 0)
    def _():
        state_ref[...] = jnp.zeros_like(state_ref)

    c = RET_CHUNK
    carried = [state_ref[...]]

    def chunk(sl):
        q = q_ref[0, sl, :]
        k = k_ref[0, sl, :]
        v = v_ref[0, sl, :]
        s = lax.dot_general(q, k, (((1,), (1,)), ((), ())), preferred_element_type=F32) * decay_ref[0]
        inner = jnp.dot(s.astype(BF16), v, preferred_element_type=F32)
        kdec = (k.astype(F32) * kd_ref[0]).astype(BF16)
        kv = lax.dot_general(kdec, v, (((0,), (0,)), ((), ())), preferred_element_type=F32)
        yield
        for _ in range(RET_LAG - 1):
            yield
        state = carried[0]
        carried[0] = state * cd_ref[0] + kv
        o = inner + jnp.dot(q, state.astype(BF16), preferred_element_type=F32) * qd_ref[0]
        mu = jnp.mean(o, axis=-1, keepdims=True)
        oc = o - mu
        var = jnp.mean(oc * oc, axis=-1, keepdims=True)
        normed = oc * lax.rsqrt(var + LN_EPS)
        o_ref[0, sl, :] = normed.astype(BF16) * g_ref[0, sl, :]
        yield

    _run_staggered([chunk(slice(ci * c, (ci + 1) * c)) for ci in range(RET_ROWS // c)], RET_LAG + 1)
    state_ref[...] = carried[0]


def _retention(rq, rk, rv, rg, tables):
    B, S, _ = rq.shape
    c = RET_CHUNK
    qk = pl.BlockSpec((1, RET_ROWS, RET_QK_DIM), lambda b, h, r: (b, r, h))
    vv = pl.BlockSpec((1, RET_ROWS, RET_V_DIM), lambda b, h, r: (b, r, h))
    head = lambda rows, cols: pl.BlockSpec((1, rows, cols), lambda b, h, r: (h, 0, 0))
    return pl.pallas_call(
        _ret_kernel,
        grid=(B, RET_HEADS, S // RET_ROWS),
        in_specs=[qk, qk, vv, vv, head(c, c), head(c, RET_V_DIM), head(c, RET_QK_DIM),
                  head(RET_QK_DIM, RET_V_DIM)],
        out_specs=vv,
        out_shape=jax.ShapeDtypeStruct((B, S, RET_V_WIDTH), BF16),
        scratch_shapes=[pltpu.VMEM((RET_QK_DIM, RET_V_DIM), F32)],
        compiler_params=_params("parallel", "parallel", "arbitrary"),
        name="retention",
    )(rq, rk, rv, rg, *tables)


def _retention_tables():
    c = RET_CHUNK
    log_gamma = np.log1p(-np.exp2(-5.0 - np.arange(RET_HEADS, dtype=np.float64)))
    idx = np.arange(c, dtype=np.float64)
    rel = idx[:, None] - idx[None, :]
    lg = log_gamma[:, None, None]
    decay = np.where(rel[None] >= 0, np.exp(lg * np.maximum(rel, 0.0)[None]), 0.0)
    qd = np.broadcast_to(np.exp(lg * (idx + 1.0)[None, :, None]), (RET_HEADS, c, RET_V_DIM))
    kd = np.broadcast_to(np.exp(lg * (c - 1.0 - idx)[None, :, None]), (RET_HEADS, c, RET_QK_DIM))
    cd = np.broadcast_to(np.exp(lg * float(c)), (RET_HEADS, RET_QK_DIM, RET_V_DIM))
    return tuple(jnp.asarray(t, F32) for t in (decay, qd, kd, cd))


def _mix_kernel(a_ref, r_ref, gate_ref, x_ref, wsb_ref, wret_ref, wmix_ref, g_ref, b_ref, o_ref):
    for sub in range(DENSE_SUBTILES):
        rows = slice(sub * SUB_ROWS, (sub + 1) * SUB_ROWS)
        y_sb = jnp.dot(a_ref[rows, :], wsb_ref[...], preferred_element_type=F32)
        y_ret = jnp.dot(r_ref[rows, :], wret_ref[...], preferred_element_type=F32)
        merged = (gate_ref[rows, :D_MODEL] * y_sb.astype(BF16)
                  + gate_ref[rows, D_MODEL:] * y_ret.astype(BF16))
        mix = jnp.dot(merged, wmix_ref[...], preferred_element_type=F32)
        o_ref[rows, :] = _residual_norm(x_ref[rows, :], mix, g_ref[...], b_ref[...])


def _mix_ln1(attn, retg, gates, x2d, w_sb_o, w_ret_o, w_mix_o, g, b):
    T = x2d.shape[0]
    tm = SUB_ROWS * DENSE_SUBTILES
    row = lambda w: pl.BlockSpec((tm, w), lambda i: (i, 0))
    return pl.pallas_call(
        _mix_kernel,
        grid=(T // tm,),
        in_specs=[row(SB_WIDTH), row(RET_V_WIDTH), row(N_BRANCHES * D_MODEL), row(D_MODEL),
                  _resident(w_sb_o.shape), _resident(w_ret_o.shape), _resident(w_mix_o.shape),
                  _resident(g.shape), _resident(b.shape)],
        out_specs=row(D_MODEL),
        out_shape=jax.ShapeDtypeStruct((T, D_MODEL), F32),
        compiler_params=_params("parallel"),
        name="mix_ln1",
    )(attn, retg, gates, x2d, w_sb_o, w_ret_o, w_mix_o, g, b)


def _xattn_kernel(x_ref, m_ref, wkv_ref, wq_ref, wo_ref, g_ref, b_ref, o_ref, k_ref, v_ref, ctx_ref):
    @pl.when(pl.program_id(1) == 0)
    def _():
        mb = m_ref[0].astype(BF16)
        k_ref[...] = jnp.dot(mb, wkv_ref[:, :D_MODEL], preferred_element_type=F32).astype(BF16)
        v_ref[...] = jnp.dot(mb, wkv_ref[:, D_MODEL:], preferred_element_type=F32).astype(BF16)

    def group(sub):
        rows = slice(sub * SUB_ROWS, (sub + 1) * SUB_ROWS)
        x = x_ref[0, rows, :]
        q = jnp.dot(x.astype(BF16), wq_ref[...], preferred_element_type=F32).astype(BF16)
        yield
        for h in range(MEM_HEADS):
            sl = slice(h * MEM_HEAD_DIM, (h + 1) * MEM_HEAD_DIM)
            s = lax.dot_general(q[:, sl], k_ref[:, sl], (((1,), (1,)), ((), ())),
                                preferred_element_type=F32)
            p = jnp.exp2(s - jnp.max(s, axis=-1, keepdims=True))
            denom = jnp.sum(p, axis=-1, keepdims=True)
            ctx = jnp.dot(p.astype(BF16), v_ref[:, sl], preferred_element_type=F32)
            ctx_ref[rows, sl] = (ctx / denom).astype(BF16)
        yield
        xa = jnp.dot(ctx_ref[rows, :], wo_ref[...], preferred_element_type=F32)
        yield
        o_ref[0, rows, :] = _residual_norm(x, xa, g_ref[...], b_ref[...])
        yield

    _run_staggered([group(sub) for sub in range(XATTN_SUBTILES)], 4)


def _xattn_ln2(x3d, mem, w_kv, w_q, w_o, g, b):
    B, S, _ = x3d.shape
    M = mem.shape[1]
    tm = SUB_ROWS * XATTN_SUBTILES
    row = pl.BlockSpec((1, tm, D_MODEL), lambda bi, i: (bi, i, 0))
    mem_blk = pl.BlockSpec((1, M, D_MODEL), lambda bi, i: (bi, 0, 0))
    return pl.pallas_call(
        _xattn_kernel,
        grid=(B, S // tm),
        in_specs=[row, mem_blk, _resident(w_kv.shape), _resident(w_q.shape), _resident(w_o.shape),
                  _resident(g.shape), _resident(b.shape)],
        out_specs=row,
        out_shape=jax.ShapeDtypeStruct((B, S, D_MODEL), F32),
        scratch_shapes=[pltpu.VMEM((M, D_MODEL), BF16), pltpu.VMEM((M, D_MODEL), BF16),
                        pltpu.VMEM((tm, D_MODEL), BF16)],
        compiler_params=_params("parallel", "arbitrary"),
        name="xattn_ln2",
    )(x3d, mem, w_kv, w_q, w_o, g, b)


def _ffn_kernel(x_ref, win_ref, wout_ref, g_ref, b_ref, o_ref, hid_ref):
    for sub in range(FFN_ROWS // FFN_SUB_ROWS):
        rows = slice(sub * FFN_SUB_ROWS, (sub + 1) * FFN_SUB_ROWS)
        x = x_ref[rows, :]
        xb = x.astype(BF16)
        for c in range(FFN_HIDDEN // FFN_CHUNK):
            lo = c * FFN_CHUNK
            a = jnp.dot(xb, win_ref[:, lo:lo + FFN_CHUNK], preferred_element_type=F32)
            gte = jnp.dot(xb, win_ref[:, FFN_HIDDEN + lo:FFN_HIDDEN + lo + FFN_CHUNK],
                          preferred_element_type=F32)
            hid_ref[rows, lo:lo + FFN_CHUNK] = (a * _sigmoid(a) * gte).astype(BF16)
        ff = jnp.dot(hid_ref[rows, :], wout_ref[...], preferred_element_type=F32)
        o_ref[rows, :] = _residual_norm(x, ff, g_ref[...], b_ref[...])


def _ffn_ln3(x2d, w_in, w_out, g, b):
    T = x2d.shape[0]
    tm = FFN_ROWS
    row = pl.BlockSpec((tm, D_MODEL), lambda i: (i, 0))
    return pl.pallas_call(
        _ffn_kernel,
        grid=(T // tm,),
        in_specs=[row, _resident(w_in.shape), _resident(w_out.shape),
                  _resident(g.shape), _resident(b.shape)],
        out_specs=row,
        out_shape=jax.ShapeDtypeStruct((T, D_MODEL), F32),
        scratch_shapes=[pltpu.VMEM((tm, FFN_HIDDEN), BF16)],
        compiler_params=_params("parallel"),
        name="ffn_ln3",
    )(x2d, w_in, w_out, g, b)


def _rope_tables(seq):
    half = RET_QK_DIM // 2
    inv = 1.0 / (ROPE_BASE ** (np.arange(half, dtype=np.float64) / half))
    ang = np.arange(seq, dtype=np.float64)[:, None] * inv[None, :]
    cos, sin = np.cos(ang), np.sin(ang)
    return (jnp.asarray(np.concatenate([cos, cos], axis=-1), F32),
            jnp.asarray(np.concatenate([-sin, sin], axis=-1), F32))


def _cumsum_matrices():
    t = SB_TILE
    r = np.arange(t)
    upper = (r[:, None] > r[None, :]).astype(np.float32)
    ones = np.ones((t, t), np.float32)
    zero = np.zeros((t, t), np.float32)
    half = np.concatenate([upper, ones], axis=1)
    cum = np.concatenate([half, half], axis=0)
    both = lambda m: np.concatenate([np.concatenate([m, zero], axis=1),
                                     np.concatenate([zero, m], axis=1)], axis=0)
    cum3 = np.concatenate([both(upper)] + [both(ones)] * 2, axis=0)
    return jnp.asarray(cum, BF16), jnp.asarray(cum3, BF16)


def kernel(x, mem, w_in, b_gate, w_sb_o, w_ret_o, w_mix_o, ln1_g, ln1_b, w_mem_q, w_mem_kv,
           w_mem_o, ln2_g, ln2_b, w_ffn_in, w_ffn_out, ln3_g, ln3_b):
    B, S, D = x.shape
    assert D == D_MODEL and w_in.shape == (DEPTH, D_MODEL, IN_WIDTH)
    assert w_ffn_in.shape == (DEPTH, D_MODEL, 2 * FFN_HIDDEN) and mem.shape[0] == B
    for rows in (RET_ROWS, ROW_TILE, SB_TILE * SB_QTILES, SUB_ROWS * DENSE_SUBTILES,
                 SUB_ROWS * XATTN_SUBTILES, FFN_ROWS):
        assert S % rows == 0, (S, rows)
    cos_t, sin_t = _rope_tables(S)
    cum, cum3 = _cumsum_matrices()
    ret_tables = _retention_tables()
    x2d = x.reshape(B * S, D)
    for l in range(DEPTH):
        bf = lambda w: w[l].astype(BF16)
        bf_over_alpha = lambda w: (w[l] * (1.0 / DN_ALPHA)).astype(BF16)
        vec = lambda p: p[l][None, :]
        sbq, sbk, sbv, rq, rk, rv, rg, gates = _inproj(
            x2d, bf(w_in), vec(b_gate), cos_t, sin_t, S)
        seq3 = lambda a: a.reshape(B, S, a.shape[-1])
        attn = _sb_attention(seq3(sbq), seq3(sbk), seq3(sbv), cum, cum3)
        retg = _retention(seq3(rq), seq3(rk), seq3(rv), seq3(rg), ret_tables)
        x1 = _mix_ln1(attn.reshape(B * S, SB_WIDTH), retg.reshape(B * S, RET_V_WIDTH), gates, x2d,
                      bf(w_sb_o), bf(w_ret_o), bf_over_alpha(w_mix_o), vec(ln1_g), vec(ln1_b))
        w_q_scaled = (w_mem_q[l] * (MEM_HEAD_DIM ** -0.5 * LOG2_E)).astype(BF16)
        x2 = _xattn_ln2(x1.reshape(B, S, D), mem, bf(w_mem_kv), w_q_scaled, bf_over_alpha(w_mem_o),
                        vec(ln2_g), vec(ln2_b))
        x2d = _ffn_ln3(x2.reshape(B * S, D), bf(w_ffn_in), bf_over_alpha(w_ffn_out), vec(ln3_g), vec(ln3_b))
    return x2d.reshape(B, S, D)
```

```python
import jax
import jax.numpy as jnp
import numpy as np
from jax import lax
from jax.experimental import pallas as pl
from jax.experimental.pallas import tpu as pltpu

F32 = jnp.float32
BF16 = jnp.bfloat16

D_MODEL = 1024
DEPTH = 1
SB_HEADS = 8
SB_HEAD_DIM = 64
SB_WIDTH = SB_HEADS * SB_HEAD_DIM
RET_HEADS = 4
RET_QK_DIM = 128
RET_V_DIM = 256
RET_QK_WIDTH = RET_HEADS * RET_QK_DIM
RET_V_WIDTH = RET_HEADS * RET_V_DIM
ROPE_BASE = 10000.0
N_BRANCHES = 2
OFF_SB_Q = 0
OFF_SB_K = OFF_SB_Q + SB_WIDTH
OFF_SB_V = OFF_SB_K + SB_WIDTH
OFF_RET_Q = OFF_SB_V + SB_WIDTH
OFF_RET_K = OFF_RET_Q + RET_QK_WIDTH
OFF_RET_V = OFF_RET_K + RET_QK_WIDTH
OFF_RET_G = OFF_RET_V + RET_V_WIDTH
OFF_GATE = OFF_RET_G + RET_V_WIDTH
IN_WIDTH = OFF_GATE + N_BRANCHES * D_MODEL
MEM_HEADS = 4
MEM_HEAD_DIM = D_MODEL // MEM_HEADS
FFN_HIDDEN = -(-8 * D_MODEL // (3 * 256)) * 256
DN_ALPHA = (2.0 * DEPTH) ** 0.25
LN_EPS = 1e-5

LANES = 128
V7X_VMEM_BYTES = 64 * 1024 * 1024
VMEM_LIMIT_BYTES = V7X_VMEM_BYTES * 7 // 8
ROW_TILE = 512
SUB_ROWS = 256
DENSE_SUBTILES = 4
XATTN_SUBTILES = 4
MIX_COLS = 512
SB_TILE = 128
SB_QTILES = 16
SB_GROUP = 8
SB_FAST_PHASES = 5
SB_TOP_ROWS = 32
SB_FIRST_STEPS = 3
SB_STEPS_PER_CHECK = 2
SB_UNDERFLOW = -150.5
LOG2_E = 1.4426950408889634
RET_CHUNK = 256
RET_ROWS = 4096
RET_LAG = 2
FFN_CHUNK = 256
FFN_ROWS = 1024
FFN_SUB_ROWS = 256


def _resident(shape):
    zeros = (0,) * len(shape)
    return pl.BlockSpec(shape, lambda *_: zeros, pipeline_mode=pl.Buffered(1))


def _params(*sem):
    return pltpu.CompilerParams(dimension_semantics=sem, vmem_limit_bytes=VMEM_LIMIT_BYTES)


def _run_staggered(groups, n_phases):
    results = [None] * len(groups)
    for slot in range(n_phases + len(groups) - 1):
        for g, group in enumerate(groups):
            if 0 <= slot - g < n_phases:
                results[g] = next(group)
    return results


def _sigmoid(x):
    return 0.5 * jnp.tanh(0.5 * x) + 0.5


def _residual_norm(x, branch_over_alpha, g, b):
    y = x + branch_over_alpha
    mu = jnp.mean(y, axis=-1, keepdims=True)
    yc = y - mu
    var = jnp.mean(yc * yc, axis=-1, keepdims=True)
    return yc * lax.rsqrt(var + LN_EPS / (DN_ALPHA * DN_ALPHA)) * g + b


def _inproj_kernel(x_ref, w_ref, bg_ref, cos_ref, sin_ref,
                   sbq_ref, sbk_ref, sbv_ref, rq_ref, rk_ref, rv_ref, rg_ref, gate_ref):
    xb = x_ref[...].astype(BF16)

    def proj(off, width):
        return jnp.dot(xb, w_ref[:, off:off + width], preferred_element_type=F32)

    for br in range(N_BRANCHES):
        sl = slice(br * D_MODEL, (br + 1) * D_MODEL)
        hg = proj(OFF_GATE + br * D_MODEL, D_MODEL) + bg_ref[:, sl]
        gate_ref[:, sl] = _sigmoid(hg).astype(BF16)
    g = proj(OFF_RET_G, RET_V_WIDTH)
    rg_ref[...] = (g * _sigmoid(g)).astype(BF16)

    def store_head_split(h, out_ref):
        hb = h.astype(BF16)
        first = lax.broadcasted_iota(jnp.int32, (hb.shape[0], LANES), 1) < SB_HEAD_DIM
        zero = jnp.zeros((hb.shape[0], LANES), BF16)
        for p in range(SB_WIDTH // LANES):
            pair = hb[:, p * LANES:(p + 1) * LANES]
            out_ref[:, 2 * p * LANES:(2 * p + 1) * LANES] = jnp.where(first, pair, zero)
            out_ref[:, (2 * p + 1) * LANES:(2 * p + 2) * LANES] = jnp.where(first, zero, pair)

    store_head_split(proj(OFF_SB_K, SB_WIDTH), sbk_ref)
    store_head_split(proj(OFF_SB_V, SB_WIDTH), sbv_ref)

    cos = cos_ref[...]
    sin = sin_ref[...]

    def rope_store(h, out_ref, scale):
        for hd in range(RET_HEADS):
            sl = slice(hd * RET_QK_DIM, (hd + 1) * RET_QK_DIM)
            xh = h[:, sl]
            rot = xh * cos + pltpu.roll(xh, RET_QK_DIM // 2, 1) * sin
            if scale != 1.0:
                rot = rot * scale
            out_ref[:, sl] = rot.astype(BF16)

    rope_store(proj(OFF_RET_Q, RET_QK_WIDTH), rq_ref, RET_QK_DIM ** -0.5)
    rope_store(proj(OFF_RET_K, RET_QK_WIDTH), rk_ref, 1.0)
    sbq_ref[...] = (proj(OFF_SB_Q, SB_WIDTH) * (SB_HEAD_DIM ** -0.5 * LOG2_E)).astype(BF16)
    rv_ref[...] = proj(OFF_RET_V, RET_V_WIDTH).astype(BF16)


def _inproj(x2d, w_in, b_gate, cos_t, sin_t, seq):
    T = x2d.shape[0]
    tm = ROW_TILE
    tiles_per_seq = seq // tm
    row = lambda w: pl.BlockSpec((tm, w), lambda i: (i, 0))
    tab = pl.BlockSpec((tm, LANES), lambda i: (i % tiles_per_seq, 0))
    widths = (SB_WIDTH, 2 * SB_WIDTH, 2 * SB_WIDTH, RET_QK_WIDTH, RET_QK_WIDTH,
              RET_V_WIDTH, RET_V_WIDTH, N_BRANCHES * D_MODEL)
    return pl.pallas_call(
        _inproj_kernel,
        grid=(T // tm,),
        in_specs=[row(D_MODEL), _resident((D_MODEL, IN_WIDTH)),
                  _resident((1, N_BRANCHES * D_MODEL)), tab, tab],
        out_specs=[row(w) for w in widths],
        out_shape=[jax.ShapeDtypeStruct((T, w), BF16) for w in widths],
        compiler_params=_params("parallel"),
        name="inproj",
    )(x2d, w_in, b_gate, cos_t, sin_t)


def _sb_kernel(q_ref, k_ref, v_ref, cum_ref, cum3_ref, o_ref):
    blk = pl.program_id(2)
    t = SB_TILE
    cum = cum_ref[...]
    row = lax.broadcasted_iota(jnp.int32, (t, 2 * t), 0)
    col = lax.broadcasted_iota(jnp.int32, (t, 2 * t), 1)
    causal = jnp.where(col >= t, col - t, col) < row

    def stack(x):
        return jnp.concatenate([x[:, :LANES], x[:, LANES:]], axis=0)

    def kv(j):
        start = pl.multiple_of(j * t, t)
        return stack(k_ref[0, pl.ds(start, t), :]), stack(v_ref[0, pl.ds(start, t), :])

    def q_tile(u):
        return q_ref[0, u * t:(u + 1) * t, :]

    def scores(q_rows, kbd):
        return lax.dot_general(q_rows, kbd, (((1,), (1,)), ((), ())), preferred_element_type=F32)

    def logs(y, mask):
        log_beta = jnp.minimum(y, 0.0) - jnp.log2(1.0 + jnp.exp2(-jnp.abs(y)))
        log_rem = log_beta - y
        if mask is not None:
            log_rem = jnp.where(mask, log_rem, 0.0)
        hi = log_rem.astype(BF16)
        lo = (log_rem - hi.astype(F32)).astype(BF16)
        rows = [jnp.concatenate([hi[:, h * t:(h + 1) * t], lo[:, h * t:(h + 1) * t]], axis=1)
                for h in range(2)]
        return log_beta, jnp.concatenate(rows, axis=0)

    def suffix_sums(operands):
        sums = jnp.dot(jnp.concatenate(operands, axis=0), cum, preferred_element_type=F32)
        out = []
        for n in range(len(operands)):
            a = sums[2 * n * t:(2 * n + 1) * t]
            b = sums[(2 * n + 1) * t:(2 * n + 2) * t]
            out.append((jnp.concatenate([a[:, :t], b[:, :t]], axis=1),
                        jnp.concatenate([a[:, t:], b[:, t:]], axis=1)))
        return out

    base = blk * SB_QTILES

    def fast_logs(y, mask):
        log_beta = jnp.minimum(y, 0.0) - jnp.log2(1.0 + jnp.exp2(-jnp.abs(y)))
        log_rem = log_beta - y
        if mask is not None:
            log_rem = jnp.where(mask, log_rem, 0.0)
        return log_beta, log_rem.astype(BF16)

    def fast(chains):
        top = SB_TOP_ROWS
        nrows = (t, t, top)
        offsets = range(chains[0] + 1 - len(nrows), chains[-1] + 1)
        members = {d: [(d + st, st) for st in range(len(nrows)) if d + st in chains]
                   for d in offsets}
        y, values = {}, {}
        for d in offsets:
            j = base + d
            kbd, vbd = kv(jnp.maximum(j, 0))
            if d < 0:
                vbd = jnp.where(j >= 0, vbd, jnp.zeros_like(vbd))
            values[d] = vbd
            q_rows = [q_ref[0, u * t:u * t + nrows[st], :] for u, st in members[d]]
            stacked = scores(jnp.concatenate(q_rows, axis=0), kbd)
            off = 0
            for u, st in members[d]:
                y[(u, st)] = stacked[off:off + nrows[st]]
                off += nrows[st]
        yield
        lb, hi = {}, {}
        for st in range(len(nrows)):
            for u in chains:
                lb[(u, st)], hi[(u, st)] = fast_logs(y[(u, st)], causal if st == 0 else None)
        yield
        cum3 = cum3_ref[...]

        def later(st):
            n = nrows[st]
            ops = [jnp.concatenate([hi[(u, st - i)][:n] for i in range(st + 1)], axis=1)
                   for u in chains]
            out = jnp.dot(jnp.concatenate(ops, axis=0), cum3[:2 * t * (st + 1)],
                          preferred_element_type=F32)
            return {u: out[i * n:(i + 1) * n] for i, u in enumerate(chains)}

        sums = [later(st) for st in range(len(nrows))]
        yield
        w = {}
        for st in range(len(nrows)):
            for u in chains:
                e = jnp.exp2(lb[(u, st)] + sums[st][u])
                w[(u, st)] = (jnp.where(causal, e, 0.0) if st == 0 else e).astype(BF16)
        yield
        acc, acc_top = {u: None for u in chains}, {}
        for d in offsets:
            ctx = jnp.dot(jnp.concatenate([w[key] for key in members[d]], axis=0), values[d],
                          preferred_element_type=F32)
            off = 0
            for u, st in members[d]:
                part = ctx[off:off + nrows[st]]
                off += nrows[st]
                if st == len(nrows) - 1:
                    acc_top[u] = part
                else:
                    acc[u] = part if acc[u] is None else acc[u] + part
        first_cols = lambda a: jnp.maximum(a[:, 0:1], a[:, t:t + 1])
        bound = None
        for u in chains:
            b = jnp.maximum(jnp.max(first_cols(sums[2][u])), jnp.max(first_cols(sums[1][u][top:])))
            bound = b if bound is None else jnp.maximum(bound, b)
        yield bound, acc, acc_top

    def walk(s, nsteps, run, acc):
        from_diag = isinstance(s, int) and s == 0
        steps = range(nsteps)
        offsets = range(1 - nsteps, SB_QTILES)
        members = {d: [(u, st) for st in steps for u in range(SB_QTILES) if u - st == d]
                   for d in offsets}
        order = [(u, st) for st in steps for u in range(SB_QTILES)]
        is_diag = lambda st: from_diag and st == 0
        y, values = {}, {}
        for d in offsets:
            j = base + d - s
            kbd, vbd = kv(jnp.maximum(j, 0))
            if not (from_diag and d >= 0):
                vbd = jnp.where(j >= 0, vbd, jnp.zeros_like(vbd))
            values[d] = vbd
            stacked = scores(jnp.concatenate([q_tile(u) for u, _ in members[d]], axis=0), kbd)
            for n, key in enumerate(members[d]):
                y[key] = stacked[n * t:(n + 1) * t]
        parts = {key: logs(y[key], causal if is_diag(key[1]) else None) for key in order}
        sums = []
        for st in steps:
            sums.extend(suffix_sums([parts[key][1] for key in order if key[1] == st]))
        run, acc, w = list(run), list(acc), {}
        for key, (later, rowsum) in zip(order, sums):
            u = key[0]
            if is_diag(key[1]):
                w[key] = jnp.where(causal, jnp.exp2(parts[key][0] + later), 0.0).astype(BF16)
                run[u] = rowsum
            else:
                w[key] = jnp.exp2(parts[key][0] + later + run[u]).astype(BF16)
                run[u] = run[u] + rowsum
        for d in offsets:
            ctx = jnp.dot(jnp.concatenate([w[key] for key in members[d]], axis=0), values[d],
                          preferred_element_type=F32)
            for n, (u, _) in enumerate(members[d]):
                part = ctx[n * t:(n + 1) * t]
                acc[u] = part if acc[u] is None else acc[u] + part
        top = run[0]
        for u in range(1, SB_QTILES):
            top = jnp.maximum(top, run[u])
        return jnp.max(top), run, acc

    def general():
        top, run, acc = walk(0, SB_FIRST_STEPS, [None] * SB_QTILES, [None] * SB_QTILES)

        def live(carry):
            return jnp.logical_and(carry[1] > SB_UNDERFLOW, carry[0] <= base + SB_QTILES - 1)

        def body(carry):
            s = carry[0]
            top, run, acc = walk(s, SB_STEPS_PER_CHECK, carry[2:2 + SB_QTILES],
                                 carry[2 + SB_QTILES:])
            return (s + SB_STEPS_PER_CHECK, top, *run, *acc)

        out = lax.while_loop(live, body, (jnp.int32(SB_FIRST_STEPS), top, *run, *acc))
        for u in range(SB_QTILES):
            o_ref[0, u * t:(u + 1) * t, :] = out[2 + SB_QTILES + u].astype(BF16)

    groups = [fast(range(first, first + SB_GROUP)) for first in range(0, SB_QTILES, SB_GROUP)]
    results = _run_staggered(groups, SB_FAST_PHASES)
    bound, acc, acc_top = None, {}, {}
    for b, a, a_top in results:
        bound = b if bound is None else jnp.maximum(bound, b)
        acc.update(a)
        acc_top.update(a_top)
    finished = bound < SB_UNDERFLOW

    @pl.when(finished)
    def _():
        for u in range(SB_QTILES):
            lo_rows = u * t + SB_TOP_ROWS
            o_ref[0, u * t:lo_rows, :] = (acc[u][:SB_TOP_ROWS] + acc_top[u]).astype(BF16)
            o_ref[0, lo_rows:(u + 1) * t, :] = acc[u][SB_TOP_ROWS:].astype(BF16)

    @pl.when(jnp.logical_not(finished))
    def _():
        general()


def _sb_attention(q, k, v, cum, cum3):
    B, S, _ = q.shape
    rows = SB_TILE * SB_QTILES
    pairs = SB_WIDTH // LANES
    qspec = pl.BlockSpec((1, rows, LANES), lambda b, p, i: (b, i, p))
    kvspec = pl.BlockSpec((1, S, 2 * LANES), lambda b, p, i: (b, 0, p))
    return pl.pallas_call(
        _sb_kernel,
        grid=(B, pairs, S // rows),
        in_specs=[qspec, kvspec, kvspec, _resident(cum.shape), _resident(cum3.shape)],
        out_specs=qspec,
        out_shape=jax.ShapeDtypeStruct((B, S, SB_WIDTH), BF16),
        compiler_params=_params("parallel", "parallel", "arbitrary"),
        name="sb_attn",
    )(q, k, v, cum, cum3)


def _ret_kernel(q_ref, k_ref, v_ref, g_ref, decay_ref, qd_ref, kd_ref, cd_ref, o_ref, state_ref):
    @pl.when(pl.program_id(2) == 0)
    def _():
        state_ref[...] = jnp.zeros_like(state_ref)

    c = RET_CHUNK
    carried = [state_ref[...]]

    def chunk(sl):
        q = q_ref[0, sl, :]
        k = k_ref[0, sl, :]
        v = v_ref[0, sl, :]
        s = lax.dot_general(q, k, (((1,), (1,)), ((), ())), preferred_element_type=F32) * decay_ref[0]
        inner = jnp.dot(s.astype(BF16), v, preferred_element_type=F32)
        kdec = (k.astype(F32) * kd_ref[0]).astype(BF16)
        kv = lax.dot_general(kdec, v, (((0,), (0,)), ((), ())), preferred_element_type=F32)
        yield
        for _ in range(RET_LAG - 1):
            yield
        state = carried[0]
        carried[0] = state * cd_ref[0] + kv
        o = inner + jnp.dot(q, state.astype(BF16), preferred_element_type=F32) * qd_ref[0]
        mu = jnp.mean(o, axis=-1, keepdims=True)
        oc = o - mu
        var = jnp.mean(oc * oc, axis=-1, keepdims=True)
        normed = oc * lax.rsqrt(var + LN_EPS)
        o_ref[0, sl, :] = normed.astype(BF16) * g_ref[0, sl, :]
        yield

    _run_staggered([chunk(slice(ci * c, (ci + 1) * c)) for ci in range(RET_ROWS // c)], RET_LAG + 1)
    state_ref[...] = carried[0]


def _retention(rq, rk, rv, rg, tables):
    B, S, _ = rq.shape
    c = RET_CHUNK
    qk = pl.BlockSpec((1, RET_ROWS, RET_QK_DIM), lambda b, h, r: (b, r, h))
    vv = pl.BlockSpec((1, RET_ROWS, RET_V_DIM), lambda b, h, r: (b, r, h))
    head = lambda rows, cols: pl.BlockSpec((1, rows, cols), lambda b, h, r: (h, 0, 0))
    return pl.pallas_call(
        _ret_kernel,
        grid=(B, RET_HEADS, S // RET_ROWS),
        in_specs=[qk, qk, vv, vv, head(c, c), head(c, RET_V_DIM), head(c, RET_QK_DIM),
                  head(RET_QK_DIM, RET_V_DIM)],
        out_specs=vv,
        out_shape=jax.ShapeDtypeStruct((B, S, RET_V_WIDTH), BF16),
        scratch_shapes=[pltpu.VMEM((RET_QK_DIM, RET_V_DIM), F32)],
        compiler_params=_params("parallel", "parallel", "arbitrary"),
        name="retention",
    )(rq, rk, rv, rg, *tables)


def _retention_tables():
    c = RET_CHUNK
    log_gamma = np.log1p(-np.exp2(-5.0 - np.arange(RET_HEADS, dtype=np.float64)))
    idx = np.arange(c, dtype=np.float64)
    rel = idx[:, None] - idx[None, :]
    lg = log_gamma[:, None, None]
    decay = np.where(rel[None] >= 0, np.exp(lg * np.maximum(rel, 0.0)[None]), 0.0)
    qd = np.broadcast_to(np.exp(lg * (idx + 1.0)[None, :, None]), (RET_HEADS, c, RET_V_DIM))
    kd = np.broadcast_to(np.exp(lg * (c - 1.0 - idx)[None, :, None]), (RET_HEADS, c, RET_QK_DIM))
    cd = np.broadcast_to(np.exp(lg * float(c)), (RET_HEADS, RET_QK_DIM, RET_V_DIM))
    return tuple(jnp.asarray(t, F32) for t in (decay, qd, kd, cd))


def _mix_kernel(a_ref, r_ref, gate_ref, x_ref, wsb_ref, wret_ref, wmix_ref, g_ref, b_ref, o_ref):
    for sub in range(DENSE_SUBTILES):
        rows = slice(sub * SUB_ROWS, (sub + 1) * SUB_ROWS)
        halves = []
        for c in range(0, D_MODEL, MIX_COLS):
            cols = slice(c, c + MIX_COLS)
            y_sb = jnp.dot(a_ref[rows, :], wsb_ref[:, cols], preferred_element_type=F32)
            y_ret = jnp.dot(r_ref[rows, :], wret_ref[:, cols], preferred_element_type=F32)
            halves.append(gate_ref[rows, cols] * y_sb.astype(BF16)
                          + gate_ref[rows, D_MODEL + c:D_MODEL + c + MIX_COLS] * y_ret.astype(BF16))
        merged = jnp.concatenate(halves, axis=1)
        mix = jnp.dot(merged, wmix_ref[...], preferred_element_type=F32)
        o_ref[rows, :] = _residual_norm(x_ref[rows, :], mix, g_ref[...], b_ref[...])


def _mix_ln1(attn, retg, gates, x2d, w_sb_o, w_ret_o, w_mix_o, g, b):
    T = x2d.shape[0]
    tm = SUB_ROWS * DENSE_SUBTILES
    row = lambda w: pl.BlockSpec((tm, w), lambda i: (i, 0))
    return pl.pallas_call(
        _mix_kernel,
        grid=(T // tm,),
        in_specs=[row(SB_WIDTH), row(RET_V_WIDTH), row(N_BRANCHES * D_MODEL), row(D_MODEL),
                  _resident(w_sb_o.shape), _resident(w_ret_o.shape), _resident(w_mix_o.shape),
                  _resident(g.shape), _resident(b.shape)],
        out_specs=row(D_MODEL),
        out_shape=jax.ShapeDtypeStruct((T, D_MODEL), F32),
        compiler_params=_params("parallel"),
        name="mix_ln1",
    )(attn, retg, gates, x2d, w_sb_o, w_ret_o, w_mix_o, g, b)


def _xattn_kernel(x_ref, m_ref, wkv_ref, wq_ref, wo_ref, g_ref, b_ref, o_ref, k_ref, v_ref, ctx_ref):
    @pl.when(pl.program_id(1) == 0)
    def _():
        mb = m_ref[0].astype(BF16)
        k_ref[...] = jnp.dot(mb, wkv_ref[:, :D_MODEL], preferred_element_type=F32).astype(BF16)
        v_ref[...] = jnp.dot(mb, wkv_ref[:, D_MODEL:], preferred_element_type=F32).astype(BF16)

    def group(sub):
        rows = slice(sub * SUB_ROWS, (sub + 1) * SUB_ROWS)
        x = x_ref[0, rows, :]
        q = jnp.dot(x.astype(BF16), wq_ref[...], preferred_element_type=F32).astype(BF16)
        yield
        for h in range(MEM_HEADS):
            sl = slice(h * MEM_HEAD_DIM, (h + 1) * MEM_HEAD_DIM)
            s = lax.dot_general(q[:, sl], k_ref[:, sl], (((1,), (1,)), ((), ())),
                                preferred_element_type=F32)
            p = jnp.exp2(s - jnp.max(s, axis=-1, keepdims=True))
            denom = jnp.sum(p, axis=-1, keepdims=True)
            ctx = jnp.dot(p.astype(BF16), v_ref[:, sl], preferred_element_type=F32)
            ctx_ref[rows, sl] = (ctx / denom).astype(BF16)
        yield
        xa = jnp.dot(ctx_ref[rows, :], wo_ref[...], preferred_element_type=F32)
        yield
        o_ref[0, rows, :] = _residual_norm(x, xa, g_ref[...], b_ref[...])
        yield

    _run_staggered([group(sub) for sub in range(XATTN_SUBTILES)], 4)


def _xattn_ln2(x3d, mem, w_kv, w_q, w_o, g, b):
    B, S, _ = x3d.shape
    M = mem.shape[1]
    tm = SUB_ROWS * XATTN_SUBTILES
    row = pl.BlockSpec((1, tm, D_MODEL), lambda bi, i: (bi, i, 0))
    mem_blk = pl.BlockSpec((1, M, D_MODEL), lambda bi, i: (bi, 0, 0))
    return pl.pallas_call(
        _xattn_kernel,
        grid=(B, S // tm),
        in_specs=[row, mem_blk, _resident(w_kv.shape), _resident(w_q.shape), _resident(w_o.shape),
                  _resident(g.shape), _resident(b.shape)],
        out_specs=row,
        out_shape=jax.ShapeDtypeStruct((B, S, D_MODEL), F32),
        scratch_shapes=[pltpu.VMEM((M, D_MODEL), BF16), pltpu.VMEM((M, D_MODEL), BF16),
                        pltpu.VMEM((tm, D_MODEL), BF16)],
        compiler_params=_params("parallel", "arbitrary"),
        name="xattn_ln2",
    )(x3d, mem, w_kv, w_q, w_o, g, b)


def _ffn_kernel(x_ref, win_ref, wout_ref, g_ref, b_ref, o_ref, hid_ref):
    for sub in range(FFN_ROWS // FFN_SUB_ROWS):
        rows = slice(sub * FFN_SUB_ROWS, (sub + 1) * FFN_SUB_ROWS)
        x = x_ref[rows, :]
        xb = x.astype(BF16)
        for c in range(FFN_HIDDEN // FFN_CHUNK):
            lo = c * FFN_CHUNK
            a = jnp.dot(xb, win_ref[:, lo:lo + FFN_CHUNK], preferred_element_type=F32)
            gte = jnp.dot(xb, win_ref[:, FFN_HIDDEN + lo:FFN_HIDDEN + lo + FFN_CHUNK],
                          preferred_element_type=F32)
            hid_ref[rows, lo:lo + FFN_CHUNK] = (a * _sigmoid(a) * gte).astype(BF16)
        ff = jnp.dot(hid_ref[rows, :], wout_ref[...], preferred_element_type=F32)
        o_ref[rows, :] = _residual_norm(x, ff, g_ref[...], b_ref[...])


def _ffn_ln3(x2d, w_in, w_out, g, b):
    T = x2d.shape[0]
    tm = FFN_ROWS
    row = pl.BlockSpec((tm, D_MODEL), lambda i: (i, 0))
    return pl.pallas_call(
        _ffn_kernel,
        grid=(T // tm,),
        in_specs=[row, _resident(w_in.shape), _resident(w_out.shape),
                  _resident(g.shape), _resident(b.shape)],
        out_specs=row,
        out_shape=jax.ShapeDtypeStruct((T, D_MODEL), F32),
        scratch_shapes=[pltpu.VMEM((tm, FFN_HIDDEN), BF16)],
        compiler_params=_params("parallel"),
        name="ffn_ln3",
    )(x2d, w_in, w_out, g, b)


def _rope_tables(seq):
    half = RET_QK_DIM // 2
    inv = 1.0 / (ROPE_BASE ** (np.arange(half, dtype=np.float64) / half))
    ang = np.arange(seq, dtype=np.float64)[:, None] * inv[None, :]
    cos, sin = np.cos(ang), np.sin(ang)
    return (jnp.asarray(np.concatenate([cos, cos], axis=-1), F32),
            jnp.asarray(np.concatenate([-sin, sin], axis=-1), F32))


def _cumsum_matrices():
    t = SB_TILE
    r = np.arange(t)
    upper = (r[:, None] > r[None, :]).astype(np.float32)
    ones = np.ones((t, t), np.float32)
    zero = np.zeros((t, t), np.float32)
    half = np.concatenate([upper, ones], axis=1)
    cum = np.concatenate([half, half], axis=0)
    both = lambda m: np.concatenate([np.concatenate([m, zero], axis=1),
                                     np.concatenate([zero, m], axis=1)], axis=0)
    cum3 = np.concatenate([both(upper)] + [both(ones)] * 2, axis=0)
    return jnp.asarray(cum, BF16), jnp.asarray(cum3, BF16)


def kernel(x, mem, w_in, b_gate, w_sb_o, w_ret_o, w_mix_o, ln1_g, ln1_b, w_mem_q, w_mem_kv,
           w_mem_o, ln2_g, ln2_b, w_ffn_in, w_ffn_out, ln3_g, ln3_b):
    B, S, D = x.shape
    assert D == D_MODEL and w_in.shape == (DEPTH, D_MODEL, IN_WIDTH)
    assert w_ffn_in.shape == (DEPTH, D_MODEL, 2 * FFN_HIDDEN) and mem.shape[0] == B
    for rows in (RET_ROWS, ROW_TILE, SB_TILE * SB_QTILES, SUB_ROWS * DENSE_SUBTILES,
                 SUB_ROWS * XATTN_SUBTILES, FFN_ROWS):
        assert S % rows == 0, (S, rows)
    cos_t, sin_t = _rope_tables(S)
    cum, cum3 = _cumsum_matrices()
    ret_tables = _retention_tables()
    x2d = x.reshape(B * S, D)
    for l in range(DEPTH):
        bf = lambda w: w[l].astype(BF16)
        bf_over_alpha = lambda w: (w[l] * (1.0 / DN_ALPHA)).astype(BF16)
        vec = lambda p: p[l][None, :]
        sbq, sbk, sbv, rq, rk, rv, rg, gates = _inproj(
            x2d, bf(w_in), vec(b_gate), cos_t, sin_t, S)
        seq3 = lambda a: a.reshape(B, S, a.shape[-1])
        attn = _sb_attention(seq3(sbq), seq3(sbk), seq3(sbv), cum, cum3)
        retg = _retention(seq3(rq), seq3(rk), seq3(rv), seq3(rg), ret_tables)
        x1 = _mix_ln1(attn.reshape(B * S, SB_WIDTH), retg.reshape(B * S, RET_V_WIDTH), gates, x2d,
                      bf(w_sb_o), bf(w_ret_o), bf_over_alpha(w_mix_o), vec(ln1_g), vec(ln1_b))
        w_q_scaled = (w_mem_q[l] * (MEM_HEAD_DIM ** -0.5 * LOG2_E)).astype(BF16)
        x2 = _xattn_ln2(x1.reshape(B, S, D), mem, bf(w_mem_kv), w_q_scaled, bf_over_alpha(w_mem_o),
                        vec(ln2_g), vec(ln2_b))
        x2d = _ffn_ln3(x2.reshape(B * S, D), bf(w_ffn_in), bf_over_alpha(w_ffn_out), vec(ln3_g), vec(ln3_b))
    return x2d.reshape(B, S, D)
```
